```python
import math
import jax, jax.numpy as jnp
from jax import lax
import numpy as np

D_MODEL = 1024
BATCH = 4
SEQ = 4096
DEPTH = 1

ATT_GROUPS = ((128, 1), (512, 4), (2048, 16))
ATT_HEADS_PER_GROUP = 8
ATT_HEAD_DIM = 64
N_ATT_HEADS = len(ATT_GROUPS) * ATT_HEADS_PER_GROUP
ATT_QKV = N_ATT_HEADS * ATT_HEAD_DIM
ATT_WIDTH = ATT_HEADS_PER_GROUP * ATT_HEAD_DIM
ALIBI_MAX_EXP = 8.0
NEG_INF = -1e30

HGRN_EXPAND = 128
HGRN_HEADS = D_MODEL // HGRN_EXPAND
HGRN_KEY = HGRN_EXPAND
HGRN_VAL = D_MODEL // HGRN_HEADS
HGRN_FDIM = HGRN_HEADS * HGRN_KEY
HGRN_WIDTH = HGRN_HEADS * HGRN_VAL
HGRN_CHUNK = 64

D_FF = ((8 * D_MODEL + 3 * 256 - 1) // (3 * 256)) * 256

DEEPNORM_ALPHA = (2.0 * DEPTH) ** 0.25
DEEPNORM_BETA = (8.0 * DEPTH) ** -0.25
LN_EPS = 1e-5
RMS_EPS = 1e-6

IN_SPLITS = (ATT_QKV, ATT_QKV, ATT_QKV,
             HGRN_FDIM, HGRN_FDIM, HGRN_FDIM,
             HGRN_WIDTH, HGRN_WIDTH,
             D_MODEL, D_MODEL)
IN_COL_SCALE = (1.0, 1.0, DEEPNORM_BETA, 1.0, 1.0, 1.0, DEEPNORM_BETA, 1.0, 1.0, 1.0)
IN_COLS = sum(IN_SPLITS)

kernel_name = "hybrid_dilated_attn_hgrn2_deepnorm_block"


def layer_norm(x, g, b):
    xf = x.astype(jnp.float32)
    mu = jnp.mean(xf, axis=-1, keepdims=True)
    xc = xf - mu
    var = jnp.mean(xc * xc, axis=-1, keepdims=True)
    y = xc * lax.rsqrt(var + LN_EPS) * g.astype(jnp.float32) + b.astype(jnp.float32)
    return y.astype(x.dtype)


def alibi_slopes(n):
    return 2.0 ** (-ALIBI_MAX_EXP * jnp.arange(1, n + 1, dtype=jnp.float32) / n)


def banded_dilated_softmax(q, k, v, half, dil, slopes):
    n, m, h, dh = q.shape
    nb = -(-m // half)
    mp = nb * half
    qb = jnp.pad(q.astype(jnp.float32), ((0, 0), (0, mp - m), (0, 0), (0, 0))).reshape(n, nb, half, h, dh)
    pad_kv = ((0, 0), (half, mp - m + half), (0, 0), (0, 0))

    def band(t):
        tb = jnp.pad(t.astype(jnp.float32), pad_kv).reshape(n, nb + 2, half, h, dh)
        return jnp.concatenate([tb[:, :-2], tb[:, 1:-1], tb[:, 2:]], axis=2)

    kb, vb = band(k), band(v)
    s = jnp.einsum('nbqhd,nbkhd->nbhqk', qb, kb) * (dh ** -0.5)
    q_pos = jnp.arange(half)
    k_off = jnp.arange(3 * half) - half
    rel = k_off[None, :] - q_pos[:, None]
    k_idx = jnp.arange(nb)[:, None] * half + k_off[None, :]
    valid = (jnp.abs(rel) <= half)[None] & ((k_idx >= 0) & (k_idx < m))[:, None, :]
    dist = (dil * jnp.abs(rel)).astype(jnp.float32)
    bias = -slopes.astype(jnp.float32)[:, None, None] * dist[None]
    s = jnp.where(valid[None, :, None], s + bias[None, None], NEG_INF)
    mx = jnp.max(s, axis=-1, keepdims=True)
    p = jnp.exp(s - mx)
    l = jnp.sum(p, axis=-1, keepdims=True)
    o = jnp.einsum('nbhqk,nbkhd->nbqhd', p, vb) / jnp.transpose(l, (0, 1, 3, 2, 4))
    lse = jnp.transpose((mx + jnp.log(l))[..., 0], (0, 1, 3, 2))
    return o.reshape(n, mp, h, dh)[:, :m], lse.reshape(n, mp, h)[:, :m]


def dilated_group(q, k, v, window, dil, slopes):
    b, s, h, dh = q.shape
    m = s // dil
    half = window // (2 * dil)

    def to_res(t):
        return t.reshape(b, m, dil, h, dh).transpose(0, 2, 1, 3, 4).reshape(b * dil, m, h, dh)

    o, lse = banded_dilated_softmax(to_res(q), to_res(k), to_res(v), half, dil, slopes)
    o = o.reshape(b, dil, m, h, dh).transpose(0, 2, 1, 3, 4).reshape(b, s, h, dh)
    lse = lse.reshape(b, dil, m, h).transpose(0, 2, 1, 3).reshape(b, s, h)
    return o, lse


def dilated_attention(aq, ak, av, slopes):
    b, s, _ = aq.shape
    shp = (b, s, len(ATT_GROUPS), ATT_HEADS_PER_GROUP, ATT_HEAD_DIM)
    aq, ak, av = aq.reshape(shp), ak.reshape(shp), av.reshape(shp)
    outs, lses = [], []
    for g, (window, dil) in enumerate(ATT_GROUPS):
        o, lse = dilated_group(aq[:, :, g], ak[:, :, g], av[:, :, g], window, dil, slopes[g])
        outs.append(o)
        lses.append(lse)
    w = jax.nn.softmax(jnp.stack(lses), axis=0)
    o = jnp.sum(w[..., None] * jnp.stack(outs), axis=0)
    return o.reshape(b, s, ATT_WIDTH).astype(aq.dtype)


def gla_chunkwise(q, k, v, logf):
    b, h, s, kd = q.shape
    vd = v.shape[-1]
    c = HGRN_CHUNK
    n = s // c
    q = q.reshape(b, h, n, c, kd)
    k = k.reshape(b, h, n, c, kd)
    v = v.reshape(b, h, n, c, vd)
    cum = jnp.cumsum(logf.reshape(b, h, n, c, kd), axis=3)
    q_dec = q * jnp.exp(cum)
    k_inv = k * jnp.exp(-cum)
    a = jnp.einsum('bhnik,bhnjk->bhnij', q_dec, k_inv)
    causal_in_chunk = jnp.tril(jnp.ones((c, c), dtype=bool))
    a = jnp.where(causal_in_chunk, a, 0.0)
    o_intra = jnp.einsum('bhnij,bhnjv->bhniv', a, v)
    cum_last = cum[..., -1:, :]
    u = jnp.einsum('bhnck,bhncv->bhnkv', k * jnp.exp(cum_last - cum), v)
    decay = jnp.exp(cum_last[..., 0, :])

    def step(state, inp):
        d, uu = inp
        return d[..., None] * state + uu, state

    init = jnp.zeros((b, h, kd, vd), jnp.float32)
    _, s_start = lax.scan(step, init, (jnp.moveaxis(decay, 2, 0), jnp.moveaxis(u, 2, 0)))
    s_start = jnp.moveaxis(s_start, 0, 2)
    o_inter = jnp.einsum('bhnck,bhnkv->bhncv', q_dec, s_start)
    return (o_intra + o_inter).reshape(b, h, s, vd)


def hgrn2(hq, hf_fwd, hf_bwd, hi, hg, lower_bound, norm_g):
    b, s, _ = hq.shape
    H, K, V = HGRN_HEADS, HGRN_KEY, HGRN_VAL

    def heads(t, d):
        return t.astype(jnp.float32).reshape(b, s, H, d).transpose(0, 2, 1, 3)

    q = heads(jax.nn.silu(hq.astype(jnp.float32)), K) * (K ** -0.5)
    v = heads(hi, V)

    def gates(fpre, lb):
        lb = lb.reshape(H, 1, K)
        f = lb + (1.0 - lb) * jax.nn.sigmoid(heads(fpre, K))
        return 1.0 - f, jnp.log(f)

    k_f, lf_f = gates(hf_fwd, lower_bound[0])
    k_b, lf_b = gates(hf_bwd, lower_bound[1])
    flip = lambda t: jnp.flip(t, axis=2)
    o_fwd = gla_chunkwise(q, k_f, v, lf_f)
    o_bwd = flip(gla_chunkwise(flip(q), flip(k_b), flip(v), flip(lf_b)))
    o = (o_fwd + o_bwd).transpose(0, 2, 1, 3)
    o = o * lax.rsqrt(jnp.mean(o * o, axis=-1, keepdims=True) + RMS_EPS) * norm_g.astype(jnp.float32)
    o = o * jax.nn.silu(hg.astype(jnp.float32)).reshape(b, s, H, V)
    return o.reshape(b, s, HGRN_WIDTH).astype(hq.dtype)


def setup_inputs(seed: int = 0) -> dict:
    key = jax.random.key(seed)
    ks = jax.random.split(key, 20)
    f32 = jnp.float32

    def nrm(k, shape, scale):
        return jax.random.normal(k, shape, f32) * scale

    x = nrm(ks[0], (BATCH, SEQ, D_MODEL), 1.0)
    ln_in_g = 1.0 + nrm(ks[1], (D_MODEL,), 0.02)
    ln_in_b = nrm(ks[2], (D_MODEL,), 0.02)
    piece_keys = jax.random.split(ks[3], len(IN_SPLITS))
    w_in = jnp.concatenate(
        [nrm(pk, (DEPTH, D_MODEL, c), (D_MODEL ** -0.5) * sc)
         for pk, c, sc in zip(piece_keys, IN_SPLITS, IN_COL_SCALE)], axis=-1)
    hgrn_lb = nrm(ks[4], (2, DEPTH + 1, HGRN_FDIM), 0.1)
    hgrn_norm_g = 1.0 + nrm(ks[5], (DEPTH, HGRN_VAL), 0.02)
    w_att_up = nrm(ks[6], (DEPTH, ATT_WIDTH, D_MODEL), ATT_WIDTH ** -0.5)
    w_hgrn_up = nrm(ks[7], (DEPTH, HGRN_WIDTH, D_MODEL), HGRN_WIDTH ** -0.5)
    w_o = nrm(ks[8], (DEPTH, D_MODEL, D_MODEL), (D_MODEL ** -0.5) * DEEPNORM_BETA)
    ln1_g = 1.0 + nrm(ks[9], (DEPTH, D_MODEL), 0.02)
    ln1_b = nrm(ks[10], (DEPTH, D_MODEL), 0.02)
    w_ffn_in = nrm(ks[11], (DEPTH, D_MODEL, 2 * D_FF), (D_MODEL ** -0.5) * DEEPNORM_BETA)
    w_ffn_out = nrm(ks[12], (DEPTH, D_FF, D_MODEL), (D_FF ** -0.5) * DEEPNORM_BETA)
    ln2_g = 1.0 + nrm(ks[13], (DEPTH, D_MODEL), 0.02)
    ln2_b = nrm(ks[14], (DEPTH, D_MODEL), 0.02)
    return {"x": x, "ln_in_g": ln_in_g, "ln_in_b": ln_in_b, "w_in": w_in,
            "hgrn_lb": hgrn_lb, "hgrn_norm_g": hgrn_norm_g,
            "w_att_up": w_att_up, "w_hgrn_up": w_hgrn_up, "w_o": w_o,
            "ln1_g": ln1_g, "ln1_b": ln1_b, "w_ffn_in": w_ffn_in, "w_ffn_out": w_ffn_out,
            "ln2_g": ln2_g, "ln2_b": ln2_b}


def reference(x, ln_in_g, ln_in_b, w_in, hgrn_lb, hgrn_norm_g, w_att_up, w_hgrn_up, w_o,
              ln1_g, ln1_b, w_ffn_in, w_ffn_out, ln2_g, ln2_b):
    slopes = alibi_slopes(N_ATT_HEADS).reshape(len(ATT_GROUPS), ATT_HEADS_PER_GROUP)
    lower_bounds = jnp.cumsum(jax.nn.softmax(hgrn_lb.astype(jnp.float32), axis=1), axis=1)
    split_idx = [int(i) for i in np.cumsum(IN_SPLITS)[:-1]]
    h = layer_norm(x, ln_in_g, ln_in_b)
    for l in range(DEPTH):
        proj = h @ w_in[l]
        aq, ak, av, hq, hf_fwd, hf_bwd, hi, hg, ga, gh = jnp.split(proj, split_idx, axis=-1)
        att = dilated_attention(aq, ak, av, slopes)
        rec = hgrn2(hq, hf_fwd, hf_bwd, hi, hg, lower_bounds[:, l], hgrn_norm_g[l])
        merged = jax.nn.sigmoid(ga) * (att @ w_att_up[l]) + jax.nn.sigmoid(gh) * (rec @ w_hgrn_up[l])
        h = layer_norm(DEEPNORM_ALPHA * h + merged @ w_o[l], ln1_g[l], ln1_b[l])
        gate, up = jnp.split(h @ w_ffn_in[l], 2, axis=-1)
        h = layer_norm(DEEPNORM_ALPHA * h + (jax.nn.silu(gate) * up) @ w_ffn_out[l], ln2_g[l], ln2_b[l])
    return h
```

```python
import functools

import jax
import jax.numpy as jnp
from jax import lax
from jax.experimental import pallas as pl
from jax.experimental.pallas import tpu as pltpu

F32 = jnp.float32
BF16 = jnp.bfloat16

D_MODEL = 1024
DEPTH = 1
ATT_GROUPS = ((128, 1), (512, 4), (2048, 16))
N_GROUPS = len(ATT_GROUPS)
ATT_HEADS_PER_GROUP = 8
ATT_HEAD_DIM = 64
N_ATT_HEADS = N_GROUPS * ATT_HEADS_PER_GROUP
ATT_QKV = N_ATT_HEADS * ATT_HEAD_DIM
ATT_WIDTH = ATT_HEADS_PER_GROUP * ATT_HEAD_DIM
ALIBI_MAX_EXP = 8.0
NEG_INF = -1e30
HGRN_HEADS = 8
HGRN_KEY = 128
HGRN_VAL = 128
HGRN_CHUNK = 64
D_FF = 2816
DEEPNORM_ALPHA = (2.0 * DEPTH) ** 0.25
LN_EPS = 1e-5
RMS_EPS = 1e-6
QKV_COLS = 3 * ATT_QKV
REC_COLS = 3 * D_MODEL
GATE_COLS = 3 * D_MODEL
IN_COLS = QKV_COLS + REC_COLS + D_MODEL + GATE_COLS

LANES = 128
VMEM_LIMIT = 56 * 1024 * 1024
PROJ_TM = 2048
PROJ_TN = 512
ATT_BQ = 128
ATT_BK = 256
ATT_HALF = 64
MERGE_TM = 512
FFN_TM = 512
FFN_TC = 256


def _layer_norm(x, g, b):
    mu = jnp.mean(x, axis=-1, keepdims=True)
    xc = x - mu
    var = jnp.mean(xc * xc, axis=-1, keepdims=True)
    return xc * lax.rsqrt(var + LN_EPS) * g + b


def _dot(a, b):
    return jnp.dot(a, b, preferred_element_type=F32)


def _dot_nt(a, b):
    return lax.dot_general(a, b, (((1,), (1,)), ((), ())), preferred_element_type=F32)


def _dot_tn(a, b):
    return lax.dot_general(a, b, (((0,), (0,)), ((), ())), preferred_element_type=F32)


_NJ_QKV = QKV_COLS // PROJ_TN
_NJ_REC = REC_COLS // PROJ_TN
_NJ_HI = D_MODEL // PROJ_TN
_NJ_GATE = GATE_COLS // PROJ_TN
_J_REC = _NJ_QKV
_J_HI = _J_REC + _NJ_REC
_J_GATE = _J_HI + _NJ_HI
_NJ = _J_GATE + _NJ_GATE


def _ln_proj_kernel(x_ref, g_ref, b_ref, w_ref, qkv_ref, rec_ref, hi_ref, gate_ref, h_scr):
    j = pl.program_id(1)
    rows = 256

    @pl.when(j == 0)
    def _():
        def body(c, carry):
            r = pl.ds(pl.multiple_of(c * rows, rows), rows)
            h_scr[r, :] = _layer_norm(x_ref[r, :], g_ref[...], b_ref[...]).astype(BF16)
            return carry
        lax.fori_loop(0, PROJ_TM // rows, body, 0)

    @pl.when(j < _J_REC)
    def _():
        qkv_ref[...] = _dot(h_scr[...], w_ref[...]).astype(BF16)

    @pl.when((j >= _J_REC) & (j < _J_HI))
    def _():
        rec_ref[...] = _dot(h_scr[...], w_ref[...])

    @pl.when((j >= _J_HI) & (j < _J_GATE))
    def _():
        hi_ref[...] = _dot(h_scr[...], w_ref[...]).astype(BF16)

    @pl.when(j >= _J_GATE)
    def _():
        gate_ref[...] = _dot(h_scr[...], w_ref[...])


def _ln_proj(x2, g, b, w_in):
    t = x2.shape[0]
    grid = (t // PROJ_TM, _NJ)

    def clamp(j, lo, n):
        return jnp.clip(j - lo, 0, n - 1)

    return pl.pallas_call(
        _ln_proj_kernel,
        grid=grid,
        in_specs=[
            pl.BlockSpec((PROJ_TM, D_MODEL), lambda i, j: (i, 0)),
            pl.BlockSpec((1, D_MODEL), lambda i, j: (0, 0)),
            pl.BlockSpec((1, D_MODEL), lambda i, j: (0, 0)),
            pl.BlockSpec((D_MODEL, PROJ_TN), lambda i, j: (0, j)),
        ],
        out_specs=[
            pl.BlockSpec((PROJ_TM, PROJ_TN), lambda i, j: (i, clamp(j, 0, _NJ_QKV))),
            pl.BlockSpec((PROJ_TM, PROJ_TN), lambda i, j: (i, clamp(j, _J_REC, _NJ_REC))),
            pl.BlockSpec((PROJ_TM, PROJ_TN), lambda i, j: (i, clamp(j, _J_HI, _NJ_HI))),
            pl.BlockSpec((PROJ_TM, PROJ_TN), lambda i, j: (i, clamp(j, _J_GATE, _NJ_GATE))),
        ],
        out_shape=[
            jax.ShapeDtypeStruct((t, QKV_COLS), BF16),
            jax.ShapeDtypeStruct((t, REC_COLS), F32),
            jax.ShapeDtypeStruct((t, D_MODEL), BF16),
            jax.ShapeDtypeStruct((t, GATE_COLS), F32),
        ],
        scratch_shapes=[pltpu.VMEM((PROJ_TM, D_MODEL), BF16)],
        compiler_params=pltpu.CompilerParams(
            dimension_semantics=("arbitrary", "arbitrary"), vmem_limit_bytes=VMEM_LIMIT),
        name="ln_proj",
    )(x2, g, b, w_in)


def _attn_bias_tables(slopes, dil):
    ql = jnp.arange(ATT_BQ)[:, None]
    kl = jnp.arange(ATT_BK)[None, :]
    tabs = []
    for off in (0, -ATT_HALF, -2 * ATT_HALF):
        rel = kl - ql + off
        valid = jnp.abs(rel) <= ATT_HALF
        dist = (dil * jnp.abs(rel)).astype(F32)
        bias = -slopes.astype(F32)[:, None, None] * dist[None]
        tabs.append(jnp.where(valid[None], bias, NEG_INF))
    return jnp.stack(tabs)


def _attn_kernel(*refs, m, tq, has_prev, final):
    if has_prev:
        q_ref, k_ref, v_ref, tab_ref, po_ref, pl_ref = refs[:6]
        outs = refs[6:]
    else:
        q_ref, k_ref, v_ref, tab_ref = refs[:4]
        outs = refs[4:]
    qi = pl.program_id(2)
    nblk = m // ATT_BQ
    lane = lax.broadcasted_iota(jnp.int32, (1, LANES), 1)
    lo = lane < ATT_HEAD_DIM
    scale = ATT_HEAD_DIM ** -0.5
    qmask = (jnp.where(lo, scale, 0.0).astype(BF16), jnp.where(lo, 0.0, scale).astype(BF16))

    def body(i, carry):
        blk = qi * (tq // ATT_BQ) + i
        q0 = pl.multiple_of(i * ATT_BQ, ATT_BQ)
        k0 = pl.multiple_of(jnp.clip(blk * ATT_BQ - ATT_HALF, 0, m - ATT_BK), ATT_HALF)
        var = jnp.where(blk == 0, 0, jnp.where(blk == nblk - 1, 2, 1))
        rq = pl.ds(q0, ATT_BQ)
        rk = pl.ds(k0, ATT_BK)
        for pair in range(ATT_HEADS_PER_GROUP // 2):
            cs = slice(pair * LANES, (pair + 1) * LANES)
            q2 = q_ref[rq, cs]
            k2 = k_ref[rk, cs]
            v2 = v_ref[rk, cs]
            res = []
            for sub in range(2):
                s = _dot_nt(q2 * qmask[sub], k2) + tab_ref[var, 2 * pair + sub]
                mx = jnp.max(s, axis=-1, keepdims=True)
                p = jnp.exp(s - mx)
                l = jnp.sum(p, axis=-1, keepdims=True)
                o = _dot(p.astype(BF16), v2) / l
                res.append((o, mx + jnp.log(l)))
            o = jnp.where(lo, res[0][0], res[1][0])
            lse = jnp.where(lo, res[0][1], res[1][1])
            if has_prev:
                po = po_ref[rq, cs]
                pls = pl_ref[rq, cs]
                mm = jnp.maximum(pls, lse)
                e0 = jnp.exp(pls - mm)
                e1 = jnp.exp(lse - mm)
                den = e0 + e1
                o = (e0 * po + e1 * o) / den
                lse = mm + jnp.log(den)
            if final:
                outs[0][rq, cs] = o.astype(BF16)
            else:
                outs[0][rq, cs] = o
                outs[1][rq, cs] = lse
        return carry

    lax.fori_loop(0, tq // ATT_BQ, body, 0)


def _attn_group(qkv, tab, prev, g, dil, batch, seq, final):
    m = seq // dil
    tq = min(m, 1024)
    view = qkv.reshape(batch, m, dil * QKV_COLS)
    ncol = QKV_COLS // ATT_WIDTH
    grid = (batch, dil, m // tq)
    q_spec = pl.BlockSpec((None, tq, ATT_WIDTH), lambda b, r, i: (b, i, r * ncol + g))
    k_spec = pl.BlockSpec((None, m, ATT_WIDTH), lambda b, r, i: (b, 0, r * ncol + N_GROUPS + g))
    v_spec = pl.BlockSpec((None, m, ATT_WIDTH), lambda b, r, i: (b, 0, r * ncol + 2 * N_GROUPS + g))
    tab_spec = pl.BlockSpec(tab.shape, lambda b, r, i: (0, 0, 0, 0))
    o_spec = pl.BlockSpec((None, tq, ATT_WIDTH), lambda b, r, i: (b, i, r))
    in_specs = [q_spec, k_spec, v_spec, tab_spec]
    args = [view, view, view, tab]
    has_prev = prev is not None
    if has_prev:
        in_specs += [o_spec, o_spec]
        args += [prev[0].reshape(batch, m, dil * ATT_WIDTH), prev[1].reshape(batch, m, dil * ATT_WIDTH)]
    if final:
        out_specs = [o_spec]
        out_shape = [jax.ShapeDtypeStruct((batch, m, dil * ATT_WIDTH), BF16)]
    else:
        out_specs = [o_spec, o_spec]
        out_shape = [jax.ShapeDtypeStruct((batch, m, dil * ATT_WIDTH), F32)] * 2
    outs = pl.pallas_call(
        functools.partial(_attn_kernel, m=m, tq=tq, has_prev=has_prev, final=final),
        grid=grid,
        in_specs=in_specs,
        out_specs=out_specs,
        out_shape=out_shape,
        compiler_params=pltpu.CompilerParams(
            dimension_semantics=("arbitrary", "arbitrary", "arbitrary"),
            vmem_limit_bytes=VMEM_LIMIT),
        name=f"dilated_attn_g{g}",
    )(*args)
    return [o.reshape(batch * seq, ATT_WIDTH) for o in outs]


def _dilated_attention(qkv, batch, seq):
    slopes = 2.0 ** (-ALIBI_MAX_EXP * jnp.arange(1, N_ATT_HEADS + 1, dtype=F32) / N_ATT_HEADS)
    slopes = slopes.reshape(N_GROUPS, ATT_HEADS_PER_GROUP)
    prev = None
    for g, (window, dil) in enumerate(ATT_GROUPS):
        assert window // (2 * dil) == ATT_HALF
        tab = _attn_bias_tables(slopes[g], dil)
        prev = _attn_group(qkv, tab, prev, g, dil, batch, seq, final=(g == N_GROUPS - 1))
    return prev[0]


def _split3(x):
    hi = x.astype(BF16)
    r1 = x - hi.astype(F32)
    mid = r1.astype(BF16)
    lo = (r1 - mid.astype(F32)).astype(BF16)
    return hi, mid, lo


def _hgrn_kernel(hq_ref, hf_ref, hb_ref, hi_ref, hg_ref, lbp_ref, ng_ref, out_ref,
                 qd_scr, ki_scr, st_scr, carry_scr, *, seq):
    c = HGRN_CHUNK
    n_chunks = seq // c
    row = lax.broadcasted_iota(jnp.int32, (c, c), 0)
    col = lax.broadcasted_iota(jnp.int32, (c, c), 1)
    tri = (row >= col).astype(BF16)
    tri3 = jnp.concatenate([tri, tri, tri], axis=1)
    qscale = HGRN_KEY ** -0.5

    def lower_bound(d):
        a = lbp_ref[d]
        mx = jnp.max(a, axis=0, keepdims=True)
        e = jnp.exp(a - mx)
        return e[0:1] / jnp.sum(e, axis=0, keepdims=True)

    lbs = (lower_bound(0), lower_bound(1))
    carry_scr[...] = jnp.zeros_like(carry_scr)

    def gates(x_ref, r, d):
        lb = lbs[d]
        f = lb + (1.0 - lb) * jax.nn.sigmoid(x_ref[r, :])
        kk = 1.0 - f
        lf = jnp.log(f)
        hi, mid, lo = _split3(lf)
        pre = _dot(tri3, jnp.concatenate([hi, mid, lo], axis=0))
        tot = pre[c - 1:c]
        if d == 0:
            cum, cl = pre, tot
        else:
            cum = tot - pre + lf
            cl = cum[0:1]
        return kk, cum, cl

    def scan_body(n, carry):
        for d in range(2):
            ch = n if d == 0 else n_chunks - 1 - n
            r = pl.ds(pl.multiple_of(ch * c, c), c)
            xq = hq_ref[r, :]
            q = xq * jax.nn.sigmoid(xq) * qscale
            kk, cum, cl = gates(hf_ref if d == 0 else hb_ref, r, d)
            qd_scr[d, r, :] = (q * jnp.exp(cum)).astype(BF16)
            ki_scr[d, r, :] = (kk * jnp.exp(-cum)).astype(BF16)
            kd = (kk * jnp.exp(cl - cum)).astype(BF16)
            ut = _dot_tn(hi_ref[r, :], kd)
            st = carry_scr[d]
            st_scr[d, ch] = st.astype(BF16)
            carry_scr[d] = st * jnp.exp(cl) + ut
        return carry

    lax.fori_loop(0, n_chunks, scan_body, 0)

    def out_body(n, carry):
        r = pl.ds(pl.multiple_of(n * c, c), c)
        v = hi_ref[r, :]
        qf = qd_scr[0, r, :]
        qb = qd_scr[1, r, :]
        af = _dot_nt(qf, ki_scr[0, r, :])
        ab = _dot_nt(qb, ki_scr[1, r, :])
        a = jnp.where(row >= col, af, 0.0) + jnp.where(row <= col, ab, 0.0)
        o = _dot(a.astype(BF16), v) + _dot_nt(qf, st_scr[0, n]) + _dot_nt(qb, st_scr[1, n])
        o = o * lax.rsqrt(jnp.mean(o * o, axis=-1, keepdims=True) + RMS_EPS) * ng_ref[...]
        xg = hg_ref[r, :]
        out_ref[r, :] = (o * (xg * jax.nn.sigmoid(xg))).astype(BF16)
        return carry

    lax.fori_loop(0, n_chunks, out_body, 0)


def _hgrn(rec, hi, gates, hgrn_lb, norm_g, batch, seq):
    rec3 = rec.reshape(batch, seq, REC_COLS)
    hi3 = hi.reshape(batch, seq, D_MODEL)
    gates3 = gates.reshape(batch, seq, GATE_COLS)
    nh = HGRN_HEADS

    def col(k):
        return pl.BlockSpec((None, seq, LANES), lambda b, h: (b, 0, k * nh + h))

    n_chunks = seq // HGRN_CHUNK
    out = pl.pallas_call(
        functools.partial(_hgrn_kernel, seq=seq),
        grid=(batch, nh),
        in_specs=[
            col(0), col(1), col(2),
            pl.BlockSpec((None, seq, LANES), lambda b, h: (b, 0, h)),
            col(0),
            pl.BlockSpec((2, DEPTH + 1, LANES), lambda b, h: (0, 0, h)),
            pl.BlockSpec((1, HGRN_VAL), lambda b, h: (0, 0)),
        ],
        out_specs=pl.BlockSpec((None, seq, LANES), lambda b, h: (b, 0, h)),
        out_shape=jax.ShapeDtypeStruct((batch, seq, D_MODEL), BF16),
        scratch_shapes=[
            pltpu.VMEM((2, seq, HGRN_KEY), BF16),
            pltpu.VMEM((2, seq, HGRN_KEY), BF16),
            pltpu.VMEM((2, n_chunks, HGRN_VAL, HGRN_KEY), BF16),
            pltpu.VMEM((2, HGRN_VAL, HGRN_KEY), F32),
        ],
        compiler_params=pltpu.CompilerParams(
            dimension_semantics=("arbitrary", "arbitrary"), vmem_limit_bytes=VMEM_LIMIT),
        name="hgrn2",
    )(rec3, rec3, rec3, hi3, gates3, hgrn_lb, norm_g)
    return out.reshape(batch * seq, D_MODEL)


def _merge_kernel(x_ref, lg_ref, lb_ref, att_ref, rec_ref, ga_ref, gh_ref,
                  wa_ref, wr_ref, wo_ref, g1_ref, b1_ref, out_ref):
    h = _layer_norm(x_ref[...], lg_ref[...], lb_ref[...])
    ua = _dot(att_ref[...], wa_ref[...])
    ur = _dot(rec_ref[...], wr_ref[...])
    merged = jax.nn.sigmoid(ga_ref[...]) * ua + jax.nn.sigmoid(gh_ref[...]) * ur
    y = DEEPNORM_ALPHA * h + _dot(merged.astype(BF16), wo_ref[...])
    out_ref[...] = _layer_norm(y, g1_ref[...], b1_ref[...])


def _const_spec(shape):
    return pl.BlockSpec(shape, lambda i: (0,) * len(shape))


def _merge(x2, lg, lb, att, rec, gates, wa, wr, wo, g1, b1):
    t = x2.shape[0]
    tm = MERGE_TM
    return pl.pallas_call(
        _merge_kernel,
        grid=(t // tm,),
        in_specs=[
            pl.BlockSpec((tm, D_MODEL), lambda i: (i, 0)),
            _const_spec((1, D_MODEL)), _const_spec((1, D_MODEL)),
            pl.BlockSpec((tm, ATT_WIDTH), lambda i: (i, 0)),
            pl.BlockSpec((tm, D_MODEL), lambda i: (i, 0)),
            pl.BlockSpec((tm, D_MODEL), lambda i: (i, 1)),
            pl.BlockSpec((tm, D_MODEL), lambda i: (i, 2)),
            _const_spec(wa.shape), _const_spec(wr.shape), _const_spec(wo.shape),
            _const_spec((1, D_MODEL)), _const_spec((1, D_MODEL)),
        ],
        out_specs=pl.BlockSpec((tm, D_MODEL), lambda i: (i, 0)),
        out_shape=jax.ShapeDtypeStruct((t, D_MODEL), F32),
        compiler_params=pltpu.CompilerParams(
            dimension_semantics=("arbitrary",), vmem_limit_bytes=VMEM_LIMIT),
        name="merge_ln1",
    )(x2, lg, lb, att, rec, gates, gates, wa, wr, wo, g1, b1)


def _ffn_kernel(h_ref, wi_ref, wo_ref, g2_ref, b2_ref, out_ref, a_scr):
    h = h_ref[...]
    hb = h.astype(BF16)
    for c in range(D_FF // FFN_TC):
        gate = _dot(hb, wi_ref[:, c * FFN_TC:(c + 1) * FFN_TC])
        up = _dot(hb, wi_ref[:, D_FF + c * FFN_TC:D_FF + (c + 1) * FFN_TC])
        a_scr[:, c * FFN_TC:(c + 1) * FFN_TC] = (gate * jax.nn.sigmoid(gate) * up).astype(BF16)
    y = DEEPNORM_ALPHA * h + _dot(a_scr[...], wo_ref[...])
    out_ref[...] = _layer_norm(y, g2_ref[...], b2_ref[...])


def _ffn(h1, wi, wo, g2, b2):
    t = h1.shape[0]
    tm = FFN_TM
    return pl.pallas_call(
        _ffn_kernel,
        grid=(t // tm,),
        in_specs=[
            pl.BlockSpec((tm, D_MODEL), lambda i: (i, 0)),
            _const_spec(wi.shape), _const_spec(wo.shape),
            _const_spec((1, D_MODEL)), _const_spec((1, D_MODEL)),
        ],
        out_specs=pl.BlockSpec((tm, D_MODEL), lambda i: (i, 0)),
        out_shape=jax.ShapeDtypeStruct((t, D_MODEL), F32),
        scratch_shapes=[pltpu.VMEM((tm, D_FF), BF16)],
        compiler_params=pltpu.CompilerParams(
            dimension_semantics=("arbitrary",), vmem_limit_bytes=VMEM_LIMIT),
        name="ffn_ln2",
    )(h1, wi, wo, g2, b2)


def kernel(x, ln_in_g, ln_in_b, w_in, hgrn_lb, hgrn_norm_g, w_att_up, w_hgrn_up, w_o,
           ln1_g, ln1_b, w_ffn_in, w_ffn_out, ln2_g, ln2_b):
    batch, seq, d = x.shape
    assert d == D_MODEL and w_in.shape == (DEPTH, D_MODEL, IN_COLS)
    assert (batch * seq) % PROJ_TM == 0 and seq % (16 * ATT_BK) == 0
    x2 = x.reshape(batch * seq, d)
    row = lambda v: v.reshape(1, -1).astype(F32)
    lg, lb = row(ln_in_g), row(ln_in_b)

    qkv, rec_in, hi, gates = _ln_proj(x2, lg, lb, w_in[0].astype(BF16))
    att = _dilated_attention(qkv, batch, seq)
    rec = _hgrn(rec_in, hi, gates, hgrn_lb[:, :, :].astype(F32), row(hgrn_norm_g[0]), batch, seq)
    h1 = _merge(x2, lg, lb, att, rec, gates,
                w_att_up[0].astype(BF16), w_hgrn_up[0].astype(BF16), w_o[0].astype(BF16),
                row(ln1_g[0]), row(ln1_b[0]))
    out = _ffn(h1, w_ffn_in[0].astype(BF16), w_ffn_out[0].astype(BF16), row(ln2_g[0]), row(ln2_b[0]))
    return out.reshape(batch, seq, d)
```

```python
import functools

import jax
import jax.numpy as jnp
from jax import lax
from jax.experimental import pallas as pl
from jax.experimental.pallas import tpu as pltpu

F32 = jnp.float32
BF16 = jnp.bfloat16

D_MODEL = 1024
DEPTH = 1
ATT_GROUPS = ((128, 1), (512, 4), (2048, 16))
N_GROUPS = len(ATT_GROUPS)
ATT_HEADS_PER_GROUP = 8
ATT_HEAD_DIM = 64
N_ATT_HEADS = N_GROUPS * ATT_HEADS_PER_GROUP
ATT_QKV = N_ATT_HEADS * ATT_HEAD_DIM
ATT_WIDTH = ATT_HEADS_PER_GROUP * ATT_HEAD_DIM
ALIBI_MAX_EXP = 8.0
NEG_INF = -1e30
HGRN_HEADS = 8
HGRN_KEY = 128
HGRN_VAL = 128
HGRN_CHUNK = 64
D_FF = 2816
DEEPNORM_ALPHA = (2.0 * DEPTH) ** 0.25
LN_EPS = 1e-5
RMS_EPS = 1e-6
QKV_COLS = 3 * ATT_QKV
REC_COLS = 3 * D_MODEL
GATE_COLS = 3 * D_MODEL
IN_COLS = QKV_COLS + REC_COLS + D_MODEL + GATE_COLS

LANES = 128
VMEM_LIMIT = 56 * 1024 * 1024
PROJ_TM = 2048
PROJ_TN = 512
ATT_BQ = 128
ATT_BK = 256
ATT_HALF = 64
HGRN_GROUP = 8
MERGE_TM = 512
FFN_TM = 512
FFN_TC = 256


def _layer_norm(x, g, b):
    mu = jnp.mean(x, axis=-1, keepdims=True)
    xc = x - mu
    var = jnp.mean(xc * xc, axis=-1, keepdims=True)
    return xc * lax.rsqrt(var + LN_EPS) * g + b


def _dot(a, b):
    return jnp.dot(a, b, preferred_element_type=F32)


def _dot_nt(a, b):
    return lax.dot_general(a, b, (((1,), (1,)), ((), ())), preferred_element_type=F32)


def _dot_tn(a, b):
    return lax.dot_general(a, b, (((0,), (0,)), ((), ())), preferred_element_type=F32)


_NJ_QKV = QKV_COLS // PROJ_TN
_NJ_REC = REC_COLS // PROJ_TN
_NJ_HI = D_MODEL // PROJ_TN
_NJ_GATE = GATE_COLS // PROJ_TN
_J_REC = _NJ_QKV
_J_HI = _J_REC + _NJ_REC
_J_GATE = _J_HI + _NJ_HI
_NJ = _J_GATE + _NJ_GATE


def _ln_proj_kernel(x_ref, g_ref, b_ref, w_ref, qkv_ref, rec_ref, hi_ref, gate_ref, h_scr):
    j = pl.program_id(1)
    rows = 256

    @pl.when(j == 0)
    def _():
        def body(c, carry):
            r = pl.ds(pl.multiple_of(c * rows, rows), rows)
            h_scr[r, :] = _layer_norm(x_ref[r, :], g_ref[...], b_ref[...]).astype(BF16)
            return carry
        lax.fori_loop(0, PROJ_TM // rows, body, 0)

    @pl.when(j < _J_REC)
    def _():
        qkv_ref[...] = _dot(h_scr[...], w_ref[...]).astype(BF16)

    @pl.when((j >= _J_REC) & (j < _J_HI))
    def _():
        rec_ref[...] = _dot(h_scr[...], w_ref[...])

    @pl.when((j >= _J_HI) & (j < _J_GATE))
    def _():
        hi_ref[...] = _dot(h_scr[...], w_ref[...]).astype(BF16)

    @pl.when(j >= _J_GATE)
    def _():
        gate_ref[...] = _dot(h_scr[...], w_ref[...])


def _ln_proj(x2, g, b, w_in):
    t = x2.shape[0]
    grid = (t // PROJ_TM, _NJ)

    def clamp(j, lo, n):
        return jnp.clip(j - lo, 0, n - 1)

    return pl.pallas_call(
        _ln_proj_kernel,
        grid=grid,
        in_specs=[
            pl.BlockSpec((PROJ_TM, D_MODEL), lambda i, j: (i, 0)),
            pl.BlockSpec((1, D_MODEL), lambda i, j: (0, 0)),
            pl.BlockSpec((1, D_MODEL), lambda i, j: (0, 0)),
            pl.BlockSpec((D_MODEL, PROJ_TN), lambda i, j: (0, j)),
        ],
        out_specs=[
            pl.BlockSpec((PROJ_TM, PROJ_TN), lambda i, j: (i, clamp(j, 0, _NJ_QKV))),
            pl.BlockSpec((PROJ_TM, PROJ_TN), lambda i, j: (i, clamp(j, _J_REC, _NJ_REC))),
            pl.BlockSpec((PROJ_TM, PROJ_TN), lambda i, j: (i, clamp(j, _J_HI, _NJ_HI))),
            pl.BlockSpec((PROJ_TM, PROJ_TN), lambda i, j: (i, clamp(j, _J_GATE, _NJ_GATE))),
        ],
        out_shape=[
            jax.ShapeDtypeStruct((t, QKV_COLS), BF16),
            jax.ShapeDtypeStruct((t, REC_COLS), F32),
            jax.ShapeDtypeStruct((t, D_MODEL), BF16),
            jax.ShapeDtypeStruct((t, GATE_COLS), F32),
        ],
        scratch_shapes=[pltpu.VMEM((PROJ_TM, D_MODEL), BF16)],
        compiler_params=pltpu.CompilerParams(
            dimension_semantics=("arbitrary", "arbitrary"), vmem_limit_bytes=VMEM_LIMIT),
        name="ln_proj",
    )(x2, g, b, w_in)


def _attn_bias_tables(slopes, dil):
    ql = jnp.arange(ATT_BQ)[:, None]
    kl = jnp.arange(ATT_BK)[None, :]
    tabs = []
    for off in (0, -ATT_HALF, -2 * ATT_HALF):
        rel = kl - ql + off
        valid = jnp.abs(rel) <= ATT_HALF
        dist = (dil * jnp.abs(rel)).astype(F32)
        bias = -slopes.astype(F32)[:, None, None] * dist[None]
        tabs.append(jnp.where(valid[None], bias, NEG_INF))
    return jnp.stack(tabs)


def _attn_kernel(*refs, m, tq, has_prev, final):
    if has_prev:
        q_ref, k_ref, v_ref, tab_ref, po_ref, pl_ref = refs[:6]
        outs = refs[6:]
    else:
        q_ref, k_ref, v_ref, tab_ref = refs[:4]
        outs = refs[4:]
    qi = pl.program_id(2)
    nblk = m // ATT_BQ
    lane = lax.broadcasted_iota(jnp.int32, (1, LANES), 1)
    lo = lane < ATT_HEAD_DIM
    scale = ATT_HEAD_DIM ** -0.5
    qmask = (jnp.where(lo, scale, 0.0).astype(BF16), jnp.where(lo, 0.0, scale).astype(BF16))

    def body(i, carry):
        blk = qi * (tq // ATT_BQ) + i
        q0 = pl.multiple_of(i * ATT_BQ, ATT_BQ)
        k0 = pl.multiple_of(jnp.clip(blk * ATT_BQ - ATT_HALF, 0, m - ATT_BK), ATT_HALF)
        var = jnp.where(blk == 0, 0, jnp.where(blk == nblk - 1, 2, 1))
        rq = pl.ds(q0, ATT_BQ)
        rk = pl.ds(k0, ATT_BK)
        for pair in range(ATT_HEADS_PER_GROUP // 2):
            cs = slice(pair * LANES, (pair + 1) * LANES)
            q2 = q_ref[rq, cs]
            k2 = k_ref[rk, cs]
            v2 = v_ref[rk, cs]
            res = []
            for sub in range(2):
                s = _dot_nt(q2 * qmask[sub], k2) + tab_ref[var, 2 * pair + sub]
                mx = jnp.max(s, axis=-1, keepdims=True)
                p = jnp.exp(s - mx)
                l = jnp.sum(p, axis=-1, keepdims=True)
                o = _dot(p.astype(BF16), v2) / l
                res.append((o, mx + jnp.log(l)))
            o = jnp.where(lo, res[0][0], res[1][0])
            lse = jnp.where(lo, res[0][1], res[1][1])
            if has_prev:
                po = po_ref[rq, cs]
                pls = pl_ref[rq, cs]
                mm = jnp.maximum(pls, lse)
                e0 = jnp.exp(pls - mm)
                e1 = jnp.exp(lse - mm)
                den = e0 + e1
                o = (e0 * po + e1 * o) / den
                lse = mm + jnp.log(den)
            if final:
                outs[0][rq, cs] = o.astype(BF16)
            else:
                outs[0][rq, cs] = o
                outs[1][rq, cs] = lse
        return carry

    lax.fori_loop(0, tq // ATT_BQ, body, 0)


def _attn_group(qkv, tab, prev, g, dil, batch, seq, final):
    m = seq // dil
    tq = min(m, 1024)
    view = qkv.reshape(batch, m, dil * QKV_COLS)
    ncol = QKV_COLS // ATT_WIDTH
    grid = (batch, dil, m // tq)
    q_spec = pl.BlockSpec((None, tq, ATT_WIDTH), lambda b, r, i: (b, i, r * ncol + g))
    k_spec = pl.BlockSpec((None, m, ATT_WIDTH), lambda b, r, i: (b, 0, r * ncol + N_GROUPS + g))
    v_spec = pl.BlockSpec((None, m, ATT_WIDTH), lambda b, r, i: (b, 0, r * ncol + 2 * N_GROUPS + g))
    tab_spec = pl.BlockSpec(tab.shape, lambda b, r, i: (0, 0, 0, 0))
    o_spec = pl.BlockSpec((None, tq, ATT_WIDTH), lambda b, r, i: (b, i, r))
    in_specs = [q_spec, k_spec, v_spec, tab_spec]
    args = [view, view, view, tab]
    has_prev = prev is not None
    if has_prev:
        in_specs += [o_spec, o_spec]
        args += [prev[0].reshape(batch, m, dil * ATT_WIDTH), prev[1].reshape(batch, m, dil * ATT_WIDTH)]
    if final:
        out_specs = [o_spec]
        out_shape = [jax.ShapeDtypeStruct((batch, m, dil * ATT_WIDTH), BF16)]
    else:
        out_specs = [o_spec, o_spec]
        out_shape = [jax.ShapeDtypeStruct((batch, m, dil * ATT_WIDTH), F32)] * 2
    outs = pl.pallas_call(
        functools.partial(_attn_kernel, m=m, tq=tq, has_prev=has_prev, final=final),
        grid=grid,
        in_specs=in_specs,
        out_specs=out_specs,
        out_shape=out_shape,
        compiler_params=pltpu.CompilerParams(
            dimension_semantics=("arbitrary", "arbitrary", "arbitrary"),
            vmem_limit_bytes=VMEM_LIMIT),
        name=f"dilated_attn_g{g}",
    )(*args)
    return [o.reshape(batch * seq, ATT_WIDTH) for o in outs]


def _dilated_attention(qkv, batch, seq):
    slopes = 2.0 ** (-ALIBI_MAX_EXP * jnp.arange(1, N_ATT_HEADS + 1, dtype=F32) / N_ATT_HEADS)
    slopes = slopes.reshape(N_GROUPS, ATT_HEADS_PER_GROUP)
    prev = None
    for g, (window, dil) in enumerate(ATT_GROUPS):
        assert window // (2 * dil) == ATT_HALF
        tab = _attn_bias_tables(slopes[g], dil)
        prev = _attn_group(qkv, tab, prev, g, dil, batch, seq, final=(g == N_GROUPS - 1))
    return prev[0]


def _split3(x):
    hi = x.astype(BF16)
    r1 = x - hi.astype(F32)
    mid = r1.astype(BF16)
    lo = (r1 - mid.astype(F32)).astype(BF16)
    return hi, mid, lo


def _hgrn_kernel(hq_ref, hf_ref, hb_ref, hi_ref, hg_ref, lbp_ref, ng_ref, out_ref,
                 qd_scr, ki_scr, st_scr, carry_scr, *, seq):
    c = HGRN_CHUNK
    grp = HGRN_GROUP
    gr = grp * c
    n_groups = seq // gr
    row = lax.broadcasted_iota(jnp.int32, (c, c), 0)
    col = lax.broadcasted_iota(jnp.int32, (c, c), 1)
    tri = (row >= col).astype(BF16)
    tri3 = jnp.concatenate([tri, tri, tri], axis=1)
    qscale = HGRN_KEY ** -0.5

    def lower_bound(d):
        a = lbp_ref[d]
        mx = jnp.max(a, axis=0, keepdims=True)
        e = jnp.exp(a - mx)
        return e[0:1] / jnp.sum(e, axis=0, keepdims=True)

    lbs = (lower_bound(0), lower_bound(1))
    carry_scr[...] = jnp.zeros_like(carry_scr)

    def chunk_rows(x, g):
        return x[g * c:(g + 1) * c]

    def direction(d, xq, x, v, st):
        lb = lbs[d]
        q = xq * jax.nn.sigmoid(xq) * qscale
        f = lb + (1.0 - lb) * jax.nn.sigmoid(x)
        kk = 1.0 - f
        lf = jnp.log(f)
        qd, ki, kd, dec = [], [], [], []
        for g in range(grp):
            lfg, kkg = chunk_rows(lf, g), chunk_rows(kk, g)
            hi, mid, lo = _split3(lfg)
            pre = _dot(tri3, jnp.concatenate([hi, mid, lo], axis=0))
            tot = pre[c - 1:c]
            if d == 0:
                cum, cl = pre, tot
            else:
                cum = tot - pre + lfg
                cl = cum[0:1]
            qd.append((chunk_rows(q, g) * jnp.exp(cum)).astype(BF16))
            ki.append((kkg * jnp.exp(-cum)).astype(BF16))
            kd.append((kkg * jnp.exp(cl - cum)).astype(BF16))
            dec.append(jnp.exp(cl))
        entry = [None] * grp
        for g in (range(grp) if d == 0 else reversed(range(grp))):
            entry[g] = st.astype(BF16)
            st = st * dec[g] + _dot_tn(chunk_rows(v, g), kd[g])
        return jnp.concatenate(qd, axis=0), jnp.concatenate(ki, axis=0), jnp.stack(entry), st

    def scan_body(n, carry):
        nb = n_groups - 1 - n
        rf = pl.ds(pl.multiple_of(n * gr, gr), gr)
        rb = pl.ds(pl.multiple_of(nb * gr, gr), gr)
        ins = ((hq_ref[rf, :], hf_ref[rf, :], hi_ref[rf, :], carry_scr[0]),
               (hq_ref[rb, :], hb_ref[rb, :], hi_ref[rb, :], carry_scr[1]))
        outs = [direction(d, *ins[d]) for d in range(2)]
        for d, r, ng in ((0, rf, n), (1, rb, nb)):
            qd, ki, entry, st = outs[d]
            qd_scr[d, r, :] = qd
            ki_scr[d, r, :] = ki
            st_scr[d, pl.ds(ng * grp, grp)] = entry
            carry_scr[d] = st
        return carry

    lax.fori_loop(0, n_groups, scan_body, 0)

    def out_body(n, carry):
        r = pl.ds(pl.multiple_of(n * gr, gr), gr)
        v, qf, qb = hi_ref[r, :], qd_scr[0, r, :], qd_scr[1, r, :]
        kf, kb = ki_scr[0, r, :], ki_scr[1, r, :]
        sf, sb = st_scr[0, pl.ds(n * grp, grp)], st_scr[1, pl.ds(n * grp, grp)]
        xg = hg_ref[r, :]
        first = []
        for g in range(grp):
            qfg, qbg = chunk_rows(qf, g), chunk_rows(qb, g)
            af = _dot_nt(qfg, chunk_rows(kf, g))
            ab = _dot_nt(qbg, chunk_rows(kb, g))
            inter = _dot_nt(jnp.concatenate([qfg, qbg], axis=1), jnp.concatenate([sf[g], sb[g]], axis=1))
            first.append((af, ab, inter))
        os_ = []
        for g, (af, ab, inter) in enumerate(first):
            a = jnp.where(row >= col, af, 0.0) + jnp.where(row <= col, ab, 0.0)
            os_.append(_dot(a.astype(BF16), chunk_rows(v, g)) + inter)
        o = jnp.concatenate(os_, axis=0)
        o = o * lax.rsqrt(jnp.mean(o * o, axis=-1, keepdims=True) + RMS_EPS) * ng_ref[...]
        out_ref[r, :] = (o * (xg * jax.nn.sigmoid(xg))).astype(BF16)
        return carry

    lax.fori_loop(0, n_groups, out_body, 0)


def _hgrn(rec, hi, gates, hgrn_lb, norm_g, batch, seq):
    rec3 = rec.reshape(batch, seq, REC_COLS)
    hi3 = hi.reshape(batch, seq, D_MODEL)
    gates3 = gates.reshape(batch, seq, GATE_COLS)
    nh = HGRN_HEADS

    def col(k):
        return pl.BlockSpec((None, seq, LANES), lambda b, h: (b, 0, k * nh + h))

    n_chunks = seq // HGRN_CHUNK
    out = pl.pallas_call(
        functools.partial(_hgrn_kernel, seq=seq),
        grid=(batch, nh),
        in_specs=[
            col(0), col(1), col(2),
            pl.BlockSpec((None, seq, LANES), lambda b, h: (b, 0, h)),
            col(0),
            pl.BlockSpec((2, DEPTH + 1, LANES), lambda b, h: (0, 0, h)),
            pl.BlockSpec((1, HGRN_VAL), lambda b, h: (0, 0)),
        ],
        out_specs=pl.BlockSpec((None, seq, LANES), lambda b, h: (b, 0, h)),
        out_shape=jax.ShapeDtypeStruct((batch, seq, D_MODEL), BF16),
        scratch_shapes=[
            pltpu.VMEM((2, seq, HGRN_KEY), BF16),
            pltpu.VMEM((2, seq, HGRN_KEY), BF16),
            pltpu.VMEM((2, n_chunks, HGRN_VAL, HGRN_KEY), BF16),
            pltpu.VMEM((2, HGRN_VAL, HGRN_KEY), F32),
        ],
        compiler_params=pltpu.CompilerParams(
            dimension_semantics=("arbitrary", "arbitrary"), vmem_limit_bytes=VMEM_LIMIT),
        name="hgrn2",
    )(rec3, rec3, rec3, hi3, gates3, hgrn_lb, norm_g)
    return out.reshape(batch * seq, D_MODEL)


def _merge_kernel(x_ref, lg_ref, lb_ref, att_ref, rec_ref, ga_ref, gh_ref,
                  wa_ref, wr_ref, wo_ref, g1_ref, b1_ref, out_ref):
    h = _layer_norm(x_ref[...], lg_ref[...], lb_ref[...])
    ua = _dot(att_ref[...], wa_ref[...])
    ur = _dot(rec_ref[...], wr_ref[...])
    merged = jax.nn.sigmoid(ga_ref[...]) * ua + jax.nn.sigmoid(gh_ref[...]) * ur
    y = DEEPNORM_ALPHA * h + _dot(merged.astype(BF16), wo_ref[...])
    out_ref[...] = _layer_norm(y, g1_ref[...], b1_ref[...])


def _const_spec(shape):
    return pl.BlockSpec(shape, lambda i: (0,) * len(shape))


def _merge(x2, lg, lb, att, rec, gates, wa, wr, wo, g1, b1):
    t = x2.shape[0]
    tm = MERGE_TM
    return pl.pallas_call(
        _merge_kernel,
        grid=(t // tm,),
        in_specs=[
            pl.BlockSpec((tm, D_MODEL), lambda i: (i, 0)),
            _const_spec((1, D_MODEL)), _const_spec((1, D_MODEL)),
            pl.BlockSpec((tm, ATT_WIDTH), lambda i: (i, 0)),
            pl.BlockSpec((tm, D_MODEL), lambda i: (i, 0)),
            pl.BlockSpec((tm, D_MODEL), lambda i: (i, 1)),
            pl.BlockSpec((tm, D_MODEL), lambda i: (i, 2)),
            _const_spec(wa.shape), _const_spec(wr.shape), _const_spec(wo.shape),
            _const_spec((1, D_MODEL)), _const_spec((1, D_MODEL)),
        ],
        out_specs=pl.BlockSpec((tm, D_MODEL), lambda i: (i, 0)),
        out_shape=jax.ShapeDtypeStruct((t, D_MODEL), F32),
        compiler_params=pltpu.CompilerParams(
            dimension_semantics=("arbitrary",), vmem_limit_bytes=VMEM_LIMIT),
        name="merge_ln1",
    )(x2, lg, lb, att, rec, gates, gates, wa, wr, wo, g1, b1)


def _ffn_kernel(h_ref, wi_ref, wo_ref, g2_ref, b2_ref, out_ref, a_scr):
    h = h_ref[...]
    hb = h.astype(BF16)
    for c in range(D_FF // FFN_TC):
        gate = _dot(hb, wi_ref[:, c * FFN_TC:(c + 1) * FFN_TC])
        up = _dot(hb, wi_ref[:, D_FF + c * FFN_TC:D_FF + (c + 1) * FFN_TC])
        a_scr[:, c * FFN_TC:(c + 1) * FFN_TC] = (gate * jax.nn.sigmoid(gate) * up).astype(BF16)
    y = DEEPNORM_ALPHA * h + _dot(a_scr[...], wo_ref[...])
    out_ref[...] = _layer_norm(y, g2_ref[...], b2_ref[...])


def _ffn(h1, wi, wo, g2, b2):
    t = h1.shape[0]
    tm = FFN_TM
    return pl.pallas_call(
        _ffn_kernel,
        grid=(t // tm,),
        in_specs=[
            pl.BlockSpec((tm, D_MODEL), lambda i: (i, 0)),
            _const_spec(wi.shape), _const_spec(wo.shape),
            _const_spec((1, D_MODEL)), _const_spec((1, D_MODEL)),
        ],
        out_specs=pl.BlockSpec((tm, D_MODEL), lambda i: (i, 0)),
        out_shape=jax.ShapeDtypeStruct((t, D_MODEL), F32),
        scratch_shapes=[pltpu.VMEM((tm, D_FF), BF16)],
        compiler_params=pltpu.CompilerParams(
            dimension_semantics=("arbitrary",), vmem_limit_bytes=VMEM_LIMIT),
        name="ffn_ln2",
    )(h1, wi, wo, g2, b2)


def kernel(x, ln_in_g, ln_in_b, w_in, hgrn_lb, hgrn_norm_g, w_att_up, w_hgrn_up, w_o,
           ln1_g, ln1_b, w_ffn_in, w_ffn_out, ln2_g, ln2_b):
    batch, seq, d = x.shape
    assert d == D_MODEL and w_in.shape == (DEPTH, D_MODEL, IN_COLS)
    assert (batch * seq) % PROJ_TM == 0 and seq % (16 * ATT_BK) == 0
    x2 = x.reshape(batch * seq, d)
    row = lambda v: v.reshape(1, -1).astype(F32)
    lg, lb = row(ln_in_g), row(ln_in_b)

    qkv, rec_in, hi, gates = _ln_proj(x2, lg, lb, w_in[0].astype(BF16))
    att = _dilated_attention(qkv, batch, seq)
    rec = _hgrn(rec_in, hi, gates, hgrn_lb[:, :, :].astype(F32), row(hgrn_norm_g[0]), batch, seq)
    h1 = _merge(x2, lg, lb, att, rec, gates,
                w_att_up[0].astype(BF16), w_hgrn_up[0].astype(BF16), w_o[0].astype(BF16),
                row(ln1_g[0]), row(ln1_b[0]))
    out = _ffn(h1, w_ffn_in[0].astype(BF16), w_ffn_out[0].astype(BF16), row(ln2_g[0]), row(ln2_b[0]))
    return out.reshape(batch, seq, d)
```

```python
import functools

import jax
import jax.numpy as jnp
from jax import lax
from jax.experimental import pallas as pl
from jax.experimental.pallas import tpu as pltpu

F32 = jnp.float32
BF16 = jnp.bfloat16

D_MODEL = 1024
DEPTH = 1
ATT_GROUPS = ((128, 1), (512, 4), (2048, 16))
N_GROUPS = len(ATT_GROUPS)
ATT_HEADS_PER_GROUP = 8
ATT_HEAD_DIM = 64
N_ATT_HEADS = N_GROUPS * ATT_HEADS_PER_GROUP
ATT_QKV = N_ATT_HEADS * ATT_HEAD_DIM
ATT_WIDTH = ATT_HEADS_PER_GROUP * ATT_HEAD_DIM
ALIBI_MAX_EXP = 8.0
NEG_INF = -1e30
HGRN_HEADS = 8
HGRN_KEY = 128
HGRN_VAL = 128
HGRN_CHUNK = 64
D_FF = 2816
DEEPNORM_ALPHA = (2.0 * DEPTH) ** 0.25
LN_EPS = 1e-5
RMS_EPS = 1e-6
QKV_COLS = 3 * ATT_QKV
REC_COLS = 3 * D_MODEL
GATE_COLS = 3 * D_MODEL
IN_COLS = QKV_COLS + REC_COLS + D_MODEL + GATE_COLS

LANES = 128
VMEM_LIMIT = 56 * 1024 * 1024
PROJ_TM = 2048
PROJ_TN = 512
ATT_BQ = 128
ATT_BK = 256
ATT_HALF = 64
HGRN_GROUP = 8
MERGE_TM = 512
FFN_TM = 512
FFN_TC = 256


def _layer_norm(x, g, b):
    mu = jnp.mean(x, axis=-1, keepdims=True)
    xc = x - mu
    var = jnp.mean(xc * xc, axis=-1, keepdims=True)
    return xc * lax.rsqrt(var + LN_EPS) * g + b


def _dot(a, b):
    return jnp.dot(a, b, preferred_element_type=F32)


def _dot_nt(a, b):
    return lax.dot_general(a, b, (((1,), (1,)), ((), ())), preferred_element_type=F32)


def _dot_tn(a, b):
    return lax.dot_general(a, b, (((0,), (0,)), ((), ())), preferred_element_type=F32)


_PROJ_WCOL = (0, 3, 6, 15, 16, 1, 4, 7, 2, 5, 8, 9, 10, 11, 12, 13, 14, 17, 18)
_NJ_A, _NJ_B, _NJ_C, _NJ_D = 5, 3, 3, 8
_J_B = _NJ_A
_J_C = _J_B + _NJ_B
_J_D = _J_C + _NJ_C
_NJ = _J_D + _NJ_D
A_COLS = _NJ_A * PROJ_TN
D_COLS = _NJ_D * PROJ_TN
GATE_COL0 = QKV_COLS + REC_COLS + 2 * D_MODEL


def _ln_proj_kernel(wcol_ref, x_ref, g_ref, b_ref, w_ref, a_ref, b_out_ref, c_out_ref, d_ref,
                    h_scr, acc_scr):
    del wcol_ref
    j = pl.program_id(1)
    rows = 256

    @pl.when(j == 0)
    def _():
        def body(c, carry):
            r = pl.ds(pl.multiple_of(c * rows, rows), rows)
            h_scr[r, :] = _layer_norm(x_ref[r, :], g_ref[...], b_ref[...]).astype(BF16)
            return carry
        lax.fori_loop(0, PROJ_TM // rows, body, 0)

    @pl.when(j < _J_B)
    def _():
        a_ref[...] = _dot(h_scr[...], w_ref[...]).astype(BF16)

    def stream_major(out_ref, dil):
        acc = _dot(h_scr[...], w_ref[...])
        n = PROJ_TM // dil
        for c in range(PROJ_TN // LANES):
            cs = slice(c * LANES, (c + 1) * LANES)
            acc_scr[c] = acc[:, cs]
            for r in range(dil):
                out_ref[r, :, cs] = acc_scr[c, pl.ds(r, n, stride=dil), :].astype(BF16)

    @pl.when((j >= _J_B) & (j < _J_C))
    def _():
        stream_major(b_out_ref, ATT_GROUPS[1][1])

    @pl.when((j >= _J_C) & (j < _J_D))
    def _():
        stream_major(c_out_ref, ATT_GROUPS[2][1])

    @pl.when(j >= _J_D)
    def _():
        d_ref[...] = _dot(h_scr[...], w_ref[...])


def _ln_proj(x2, g, b, w_in, batch, seq):
    t = x2.shape[0]
    tpb = seq // PROJ_TM
    d1, d2 = ATT_GROUPS[1][1], ATT_GROUPS[2][1]
    wcol = jnp.asarray(_PROJ_WCOL, jnp.int32)

    def clamp(j, lo, n):
        return jnp.clip(j - lo, 0, n - 1)

    def stream_spec(dil, lo, n):
        return pl.BlockSpec((None, dil, PROJ_TM // dil, PROJ_TN),
                            lambda i, j, wc: (i // tpb, 0, i % tpb, clamp(j, lo, n)))

    grid_spec = pltpu.PrefetchScalarGridSpec(
        num_scalar_prefetch=1,
        grid=(t // PROJ_TM, _NJ),
        in_specs=[
            pl.BlockSpec((PROJ_TM, D_MODEL), lambda i, j, wc: (i, 0)),
            pl.BlockSpec((1, D_MODEL), lambda i, j, wc: (0, 0)),
            pl.BlockSpec((1, D_MODEL), lambda i, j, wc: (0, 0)),
            pl.BlockSpec((D_MODEL, PROJ_TN), lambda i, j, wc: (0, wc[j])),
        ],
        out_specs=[
            pl.BlockSpec((PROJ_TM, PROJ_TN), lambda i, j, wc: (i, clamp(j, 0, _NJ_A))),
            stream_spec(d1, _J_B, _NJ_B),
            stream_spec(d2, _J_C, _NJ_C),
            pl.BlockSpec((PROJ_TM, PROJ_TN), lambda i, j, wc: (i, clamp(j, _J_D, _NJ_D))),
        ],
        scratch_shapes=[pltpu.VMEM((PROJ_TM, D_MODEL), BF16),
                        pltpu.VMEM((PROJ_TN // LANES, PROJ_TM, LANES), F32)],
    )
    return pl.pallas_call(
        _ln_proj_kernel,
        grid_spec=grid_spec,
        out_shape=[
            jax.ShapeDtypeStruct((t, A_COLS), BF16),
            jax.ShapeDtypeStruct((batch, d1, seq // d1, 3 * ATT_WIDTH), BF16),
            jax.ShapeDtypeStruct((batch, d2, seq // d2, 3 * ATT_WIDTH), BF16),
            jax.ShapeDtypeStruct((t, D_COLS), F32),
        ],
        compiler_params=pltpu.CompilerParams(
            dimension_semantics=("arbitrary", "arbitrary"), vmem_limit_bytes=VMEM_LIMIT),
        name="ln_proj",
    )(wcol, x2, g, b, w_in)


def _attn_bias_tables(slopes, dil):
    ql = jnp.arange(ATT_BQ)[:, None]
    kl = jnp.arange(ATT_BK)[None, :]
    tabs = []
    for off in (0, -ATT_HALF, -2 * ATT_HALF):
        rel = kl - ql + off
        valid = jnp.abs(rel) <= ATT_HALF
        dist = (dil * jnp.abs(rel)).astype(F32)
        bias = -slopes.astype(F32)[:, None, None] * dist[None]
        tabs.append(jnp.where(valid[None], bias, NEG_INF))
    return jnp.stack(tabs)


def _attn_kernel(q_ref, k_ref, v_ref, tab_ref, o_ref, lse_ref, *, m, tq):
    qi = pl.program_id(2)
    nblk = m // ATT_BQ
    lane = lax.broadcasted_iota(jnp.int32, (1, LANES), 1)
    lo = lane < ATT_HEAD_DIM
    scale = ATT_HEAD_DIM ** -0.5
    qmask = (jnp.where(lo, scale, 0.0).astype(BF16), jnp.where(lo, 0.0, scale).astype(BF16))

    def body(i, carry):
        blk = qi * (tq // ATT_BQ) + i
        q0 = pl.multiple_of(i * ATT_BQ, ATT_BQ)
        k0 = pl.multiple_of(jnp.clip(blk * ATT_BQ - ATT_HALF, 0, m - ATT_BK), ATT_HALF)
        var = jnp.where(blk == 0, 0, jnp.where(blk == nblk - 1, 2, 1))
        rq = pl.ds(q0, ATT_BQ)
        rk = pl.ds(k0, ATT_BK)
        for pair in range(ATT_HEADS_PER_GROUP // 2):
            cs = slice(pair * LANES, (pair + 1) * LANES)
            q2 = q_ref[rq, cs]
            k2 = k_ref[rk, cs]
            v2 = v_ref[rk, cs]
            res = []
            for sub in range(2):
                s = _dot_nt(q2 * qmask[sub], k2) + tab_ref[var, 2 * pair + sub]
                mx = jnp.max(s, axis=-1, keepdims=True)
                p = jnp.exp(s - mx)
                l = jnp.sum(p, axis=-1, keepdims=True)
                o = _dot(p.astype(BF16), v2) / l
                res.append((o, mx + jnp.log(l)))
            o_ref[rq, cs] = jnp.where(lo, res[0][0], res[1][0])
            lse_ref[rq, cs] = jnp.where(lo, res[0][1], res[1][1])
        return carry

    lax.fori_loop(0, tq // ATT_BQ, body, 0)


def _attn_group(qkv4, tab, g, dil, batch, seq):
    m = seq // dil
    tq = min(m, 1024)
    q_spec = pl.BlockSpec((None, None, tq, ATT_WIDTH), lambda b, r, i: (b, r, i, 0))
    k_spec = pl.BlockSpec((None, None, m, ATT_WIDTH), lambda b, r, i: (b, r, 0, 1))
    v_spec = pl.BlockSpec((None, None, m, ATT_WIDTH), lambda b, r, i: (b, r, 0, 2))
    tab_spec = pl.BlockSpec(tab.shape, lambda b, r, i: (0, 0, 0, 0))
    out_shape = jax.ShapeDtypeStruct((batch, dil, m, ATT_WIDTH), F32)
    return pl.pallas_call(
        functools.partial(_attn_kernel, m=m, tq=tq),
        grid=(batch, dil, m // tq),
        in_specs=[q_spec, k_spec, v_spec, tab_spec],
        out_specs=[q_spec, q_spec],
        out_shape=[out_shape, out_shape],
        compiler_params=pltpu.CompilerParams(
            dimension_semantics=("arbitrary", "arbitrary", "arbitrary"),
            vmem_limit_bytes=VMEM_LIMIT),
        name=f"dilated_attn_g{g}",
    )(qkv4, qkv4, qkv4, tab)


def _dilated_attention(qkv_groups, batch, seq):
    slopes = 2.0 ** (-ALIBI_MAX_EXP * jnp.arange(1, N_ATT_HEADS + 1, dtype=F32) / N_ATT_HEADS)
    slopes = slopes.reshape(N_GROUPS, ATT_HEADS_PER_GROUP)
    outs = []
    for g, (window, dil) in enumerate(ATT_GROUPS):
        assert window // (2 * dil) == ATT_HALF
        outs.append(_attn_group(qkv_groups[g], _attn_bias_tables(slopes[g], dil), g, dil, batch, seq))
    return outs


def _split3(x):
    hi = x.astype(BF16)
    r1 = x - hi.astype(F32)
    mid = r1.astype(BF16)
    lo = (r1 - mid.astype(F32)).astype(BF16)
    return hi, mid, lo


def _hgrn_kernel(hq_ref, hf_ref, hb_ref, hi_ref, hg_ref, lbp_ref, ng_ref, out_ref,
                 qd_scr, ki_scr, st_scr, carry_scr, *, seq):
    c = HGRN_CHUNK
    grp = HGRN_GROUP
    gr = grp * c
    n_groups = seq // gr
    row = lax.broadcasted_iota(jnp.int32, (c, c), 0)
    col = lax.broadcasted_iota(jnp.int32, (c, c), 1)
    tri = (row >= col).astype(BF16)
    tri3 = jnp.concatenate([tri, tri, tri], axis=1)
    qscale = HGRN_KEY ** -0.5

    def lower_bound(d):
        a = lbp_ref[d]
        mx = jnp.max(a, axis=0, keepdims=True)
        e = jnp.exp(a - mx)
        return e[0:1] / jnp.sum(e, axis=0, keepdims=True)

    lbs = (lower_bound(0), lower_bound(1))
    carry_scr[...] = jnp.zeros_like(carry_scr)

    def chunk_rows(x, g):
        return x[g * c:(g + 1) * c]

    def direction(d, xq, x, v, st):
        lb = lbs[d]
        q = xq * jax.nn.sigmoid(xq) * qscale
        f = lb + (1.0 - lb) * jax.nn.sigmoid(x)
        kk = 1.0 - f
        lf = jnp.log(f)
        qd, ki, kd, dec = [], [], [], []
        for g in range(grp):
            lfg, kkg = chunk_rows(lf, g), chunk_rows(kk, g)
            hi, mid, lo = _split3(lfg)
            pre = _dot(tri3, jnp.concatenate([hi, mid, lo], axis=0))
            tot = pre[c - 1:c]
            if d == 0:
                cum, cl = pre, tot
            else:
                cum = tot - pre + lfg
                cl = cum[0:1]
            qd.append((chunk_rows(q, g) * jnp.exp(cum)).astype(BF16))
            ki.append((kkg * jnp.exp(-cum)).astype(BF16))
            kd.append((kkg * jnp.exp(cl - cum)).astype(BF16))
            dec.append(jnp.exp(cl))
        entry = [None] * grp
        for g in (range(grp) if d == 0 else reversed(range(grp))):
            entry[g] = st.astype(BF16)
            st = st * dec[g] + _dot_tn(chunk_rows(v, g), kd[g])
        return jnp.concatenate(qd, axis=0), jnp.concatenate(ki, axis=0), jnp.stack(entry), st

    def scan_body(n, carry):
        nb = n_groups - 1 - n
        rf = pl.ds(pl.multiple_of(n * gr, gr), gr)
        rb = pl.ds(pl.multiple_of(nb * gr, gr), gr)
        ins = ((hq_ref[rf, :], hf_ref[rf, :], hi_ref[rf, :], carry_scr[0]),
               (hq_ref[rb, :], hb_ref[rb, :], hi_ref[rb, :], carry_scr[1]))
        outs = [direction(d, *ins[d]) for d in range(2)]
        for d, r, ng in ((0, rf, n), (1, rb, nb)):
            qd, ki, entry, st = outs[d]
            qd_scr[d, r, :] = qd
            ki_scr[d, r, :] = ki
            st_scr[d, pl.ds(ng * grp, grp)] = entry
            carry_scr[d] = st
        return carry

    lax.fori_loop(0, n_groups, scan_body, 0)

    def out_body(n, carry):
        r = pl.ds(pl.multiple_of(n * gr, gr), gr)
        v, qf, qb = hi_ref[r, :], qd_scr[0, r, :], qd_scr[1, r, :]
        kf, kb = ki_scr[0, r, :], ki_scr[1, r, :]
        sf, sb = st_scr[0, pl.ds(n * grp, grp)], st_scr[1, pl.ds(n * grp, grp)]
        xg = hg_ref[r, :]
        first = []
        for g in range(grp):
            qfg, qbg = chunk_rows(qf, g), chunk_rows(qb, g)
            af = _dot_nt(qfg, chunk_rows(kf, g))
            ab = _dot_nt(qbg, chunk_rows(kb, g))
            inter = _dot_nt(jnp.concatenate([qfg, qbg], axis=1), jnp.concatenate([sf[g], sb[g]], axis=1))
            first.append((af, ab, inter))
        os_ = []
        for g, (af, ab, inter) in enumerate(first):
            a = jnp.where(row >= col, af, 0.0) + jnp.where(row <= col, ab, 0.0)
            os_.append(_dot(a.astype(BF16), chunk_rows(v, g)) + inter)
        o = jnp.concatenate(os_, axis=0)
        o = o * lax.rsqrt(jnp.mean(o * o, axis=-1, keepdims=True) + RMS_EPS) * ng_ref[...]
        out_ref[r, :] = (o * (xg * jax.nn.sigmoid(xg))).astype(BF16)
        return carry

    lax.fori_loop(0, n_groups, out_body, 0)


def _hgrn(proj_a, proj_d, hgrn_lb, norm_g, batch, seq):
    a3 = proj_a.reshape(batch, seq, A_COLS)
    d3 = proj_d.reshape(batch, seq, D_COLS)
    nh = HGRN_HEADS
    hi_col0 = QKV_COLS // N_GROUPS // LANES

    def col(k):
        return pl.BlockSpec((None, seq, LANES), lambda b, h: (b, 0, k * nh + h))

    n_chunks = seq // HGRN_CHUNK
    out = pl.pallas_call(
        functools.partial(_hgrn_kernel, seq=seq),
        grid=(batch, nh),
        in_specs=[
            col(0), col(1), col(2),
            pl.BlockSpec((None, seq, LANES), lambda b, h: (b, 0, hi_col0 + h)),
            col(3),
            pl.BlockSpec((2, DEPTH + 1, LANES), lambda b, h: (0, 0, h)),
            pl.BlockSpec((1, HGRN_VAL), lambda b, h: (0, 0)),
        ],
        out_specs=pl.BlockSpec((None, seq, LANES), lambda b, h: (b, 0, h)),
        out_shape=jax.ShapeDtypeStruct((batch, seq, D_MODEL), BF16),
        scratch_shapes=[
            pltpu.VMEM((2, seq, HGRN_KEY), BF16),
            pltpu.VMEM((2, seq, HGRN_KEY), BF16),
            pltpu.VMEM((2, n_chunks, HGRN_VAL, HGRN_KEY), BF16),
            pltpu.VMEM((2, HGRN_VAL, HGRN_KEY), F32),
        ],
        compiler_params=pltpu.CompilerParams(
            dimension_semantics=("arbitrary", "arbitrary"), vmem_limit_bytes=VMEM_LIMIT),
        name="hgrn2",
    )(d3, d3, d3, a3, d3, hgrn_lb, norm_g)
    return out.reshape(batch * seq, D_MODEL)


def _merge_kernel(x_ref, lg_ref, lb_ref, o0_ref, l0_ref, o1_ref, l1_ref, o2_ref, l2_ref, rec_ref,
                  wg_ref, wa_ref, wr_ref, wo_ref, g1_ref, b1_ref, out_ref,
                  o1_scr, l1_scr, o2_scr, l2_scr):
    tm = MERGE_TM
    for src, dst, dil in ((o1_ref, o1_scr, ATT_GROUPS[1][1]), (l1_ref, l1_scr, ATT_GROUPS[1][1]),
                          (o2_ref, o2_scr, ATT_GROUPS[2][1]), (l2_ref, l2_scr, ATT_GROUPS[2][1])):
        for r in range(dil):
            for c in range(ATT_WIDTH // LANES):
                dst[c, pl.ds(r, tm // dil, stride=dil), :] = src[r, :, c * LANES:(c + 1) * LANES]

    def token_order(scr):
        return jnp.concatenate([scr[c] for c in range(ATT_WIDTH // LANES)], axis=1)

    l0, l1, l2 = l0_ref[...], token_order(l1_scr), token_order(l2_scr)
    mx = jnp.maximum(jnp.maximum(l0, l1), l2)
    e0, e1, e2 = jnp.exp(l0 - mx), jnp.exp(l1 - mx), jnp.exp(l2 - mx)
    att = (e0 * o0_ref[...] + e1 * token_order(o1_scr) + e2 * token_order(o2_scr)) / (e0 + e1 + e2)

    h = _layer_norm(x_ref[...], lg_ref[...], lb_ref[...])
    gates = _dot(h.astype(BF16), wg_ref[...])
    ua = _dot(att.astype(BF16), wa_ref[...])
    ur = _dot(rec_ref[...], wr_ref[...])
    merged = jax.nn.sigmoid(gates[:, :D_MODEL]) * ua + jax.nn.sigmoid(gates[:, D_MODEL:]) * ur
    y = DEEPNORM_ALPHA * h + _dot(merged.astype(BF16), wo_ref[...])
    out_ref[...] = _layer_norm(y, g1_ref[...], b1_ref[...])


def _const_spec(shape):
    return pl.BlockSpec(shape, lambda i: (0,) * len(shape))


def _merge(x2, lg, lb, att_groups, rec, wg, wa, wr, wo, g1, b1, batch, seq):
    t = x2.shape[0]
    tm = MERGE_TM
    tpb = seq // tm
    att_args, att_specs = [], []
    for (o, lse), (_, dil) in zip(att_groups, ATT_GROUPS):
        if dil == 1:
            spec = pl.BlockSpec((tm, ATT_WIDTH), lambda i: (i, 0))
            o, lse = o.reshape(t, ATT_WIDTH), lse.reshape(t, ATT_WIDTH)
        else:
            spec = pl.BlockSpec((None, dil, tm // dil, ATT_WIDTH), lambda i: (i // tpb, 0, i % tpb, 0))
        att_args += [o, lse]
        att_specs += [spec, spec]
    return pl.pallas_call(
        _merge_kernel,
        grid=(t // tm,),
        in_specs=[
            pl.BlockSpec((tm, D_MODEL), lambda i: (i, 0)),
            _const_spec((1, D_MODEL)), _const_spec((1, D_MODEL)),
            *att_specs,
            pl.BlockSpec((tm, D_MODEL), lambda i: (i, 0)),
            _const_spec(wg.shape), _const_spec(wa.shape), _const_spec(wr.shape), _const_spec(wo.shape),
            _const_spec((1, D_MODEL)), _const_spec((1, D_MODEL)),
        ],
        out_specs=pl.BlockSpec((tm, D_MODEL), lambda i: (i, 0)),
        out_shape=jax.ShapeDtypeStruct((t, D_MODEL), F32),
        scratch_shapes=[pltpu.VMEM((ATT_WIDTH // LANES, tm, LANES), F32)] * 4,
        compiler_params=pltpu.CompilerParams(
            dimension_semantics=("arbitrary",), vmem_limit_bytes=VMEM_LIMIT),
        name="merge_ln1",
    )(x2, lg, lb, *att_args, rec, wg, wa, wr, wo, g1, b1)


def _ffn_kernel(h_ref, wi_ref, wo_ref, g2_ref, b2_ref, out_ref, a_scr):
    h = h_ref[...]
    hb = h.astype(BF16)
    for c in range(D_FF // FFN_TC):
        gate = _dot(hb, wi_ref[:, c * FFN_TC:(c + 1) * FFN_TC])
        up = _dot(hb, wi_ref[:, D_FF + c * FFN_TC:D_FF + (c + 1) * FFN_TC])
        a_scr[:, c * FFN_TC:(c + 1) * FFN_TC] = (gate * jax.nn.sigmoid(gate) * up).astype(BF16)
    y = DEEPNORM_ALPHA * h + _dot(a_scr[...], wo_ref[...])
    out_ref[...] = _layer_norm(y, g2_ref[...], b2_ref[...])


def _ffn(h1, wi, wo, g2, b2):
    t = h1.shape[0]
    tm = FFN_TM
    return pl.pallas_call(
        _ffn_kernel,
        grid=(t // tm,),
        in_specs=[
            pl.BlockSpec((tm, D_MODEL), lambda i: (i, 0)),
            _const_spec(wi.shape), _const_spec(wo.shape),
            _const_spec((1, D_MODEL)), _const_spec((1, D_MODEL)),
        ],
        out_specs=pl.BlockSpec((tm, D_MODEL), lambda i: (i, 0)),
        out_shape=jax.ShapeDtypeStruct((t, D_MODEL), F32),
        scratch_shapes=[pltpu.VMEM((tm, D_FF), BF16)],
        compiler_params=pltpu.CompilerParams(
            dimension_semantics=("arbitrary",), vmem_limit_bytes=VMEM_LIMIT),
        name="ffn_ln2",
    )(h1, wi, wo, g2, b2)


def kernel(x, ln_in_g, ln_in_b, w_in, hgrn_lb, hgrn_norm_g, w_att_up, w_hgrn_up, w_o,
           ln1_g, ln1_b, w_ffn_in, w_ffn_out, ln2_g, ln2_b):
    batch, seq, d = x.shape
    assert d == D_MODEL and w_in.shape == (DEPTH, D_MODEL, IN_COLS)
    assert seq % PROJ_TM == 0 and seq % (16 * ATT_BK) == 0
    x2 = x.reshape(batch * seq, d)
    row = lambda v: v.reshape(1, -1).astype(F32)
    lg, lb = row(ln_in_g), row(ln_in_b)
    w_in_b = w_in[0].astype(BF16)

    proj_a, proj_b, proj_c, proj_d = _ln_proj(x2, lg, lb, w_in_b, batch, seq)
    att_groups = _dilated_attention(
        (proj_a.reshape(batch, 1, seq, A_COLS), proj_b, proj_c), batch, seq)
    rec = _hgrn(proj_a, proj_d, hgrn_lb.astype(F32), row(hgrn_norm_g[0]), batch, seq)
    h1 = _merge(x2, lg, lb, att_groups, rec, w_in_b[:, GATE_COL0:],
                w_att_up[0].astype(BF16), w_hgrn_up[0].astype(BF16), w_o[0].astype(BF16),
                row(ln1_g[0]), row(ln1_b[0]), batch, seq)
    out = _ffn(h1, w_ffn_in[0].astype(BF16), w_ffn_out[0].astype(BF16), row(ln2_g[0]), row(ln2_b[0]))
    return out.reshape(batch, seq, d)
```

```python
import functools

import jax
import jax.numpy as jnp
import numpy as np
from jax import lax
from jax.experimental import pallas as pl
from jax.experimental.pallas import tpu as pltpu

F32 = jnp.float32
BF16 = jnp.bfloat16

D_MODEL = 1024
DEPTH = 1
ATT_GROUPS = ((128, 1), (512, 4), (2048, 16))
N_GROUPS = len(ATT_GROUPS)
ATT_HEADS_PER_GROUP = 8
ATT_HEAD_DIM = 64
N_ATT_HEADS = N_GROUPS * ATT_HEADS_PER_GROUP
ATT_QKV = N_ATT_HEADS * ATT_HEAD_DIM
ATT_WIDTH = ATT_HEADS_PER_GROUP * ATT_HEAD_DIM
ALIBI_MAX_EXP = 8.0
NEG_INF = -1e30
HGRN_HEADS = 8
HGRN_KEY = 128
HGRN_VAL = 128
HGRN_CHUNK = 64
D_FF = 2816
DEEPNORM_ALPHA = (2.0 * DEPTH) ** 0.25
LN_EPS = 1e-5
RMS_EPS = 1e-6
QKV_COLS = 3 * ATT_QKV
REC_COLS = 3 * D_MODEL
GATE_COLS = 3 * D_MODEL
IN_COLS = QKV_COLS + REC_COLS + D_MODEL + GATE_COLS

LANES = 128
VMEM_LIMIT = 56 * 1024 * 1024
PROJ_TM = 2048
PROJ_TN = 512
ATT_BQ = 128
ATT_BK = 256
ATT_HALF = 64
HGRN_GROUP = 8
MERGE_TM = 512
MERGE_SUB = 2
FFN_TM = 512
FFN_TC = 256


def _layer_norm(x, g, b):
    mu = jnp.mean(x, axis=-1, keepdims=True)
    xc = x - mu
    var = jnp.mean(xc * xc, axis=-1, keepdims=True)
    return xc * lax.rsqrt(var + LN_EPS) * g + b


def _dot(a, b):
    return jnp.dot(a, b, preferred_element_type=F32)


def _dot_nt(a, b):
    return lax.dot_general(a, b, (((1,), (1,)), ((), ())), preferred_element_type=F32)


def _dot_tn(a, b):
    return lax.dot_general(a, b, (((0,), (0,)), ((), ())), preferred_element_type=F32)


_PROJ_STEPS = (("D", 9), ("A", 0), ("D", 10), ("A", 3), ("D", 11), ("A", 6), ("D", 12), ("A", 15),
               ("D", 13), ("A", 16), ("D", 14), ("B", 1), ("D", 17), ("B", 4), ("D", 18), ("B", 7),
               ("C", 2), ("C", 5), ("C", 8))
_NJ = len(_PROJ_STEPS)
_KINDS = "ABCD"
A_COLS = sum(k == "A" for k, _ in _PROJ_STEPS) * PROJ_TN
D_COLS = sum(k == "D" for k, _ in _PROJ_STEPS) * PROJ_TN
GATE_COL0 = QKV_COLS + REC_COLS + 2 * D_MODEL
_TAB_WCOL, _TAB_KIND, _TAB_BLOCK = 0, 1, 2


def _proj_tables():
    rows = [[w for _, w in _PROJ_STEPS], [_KINDS.index(k) for k, _ in _PROJ_STEPS]]
    for kind in _KINDS:
        n, blocks = 0, []
        for k, _ in _PROJ_STEPS:
            n += k == kind
            blocks.append(max(n - 1, 0))
        rows.append(blocks)
    return np.asarray(rows, np.int32)


def _ln_proj_kernel(tab_ref, x_ref, g_ref, b_ref, w_ref, a_ref, b_out_ref, c_out_ref, d_ref,
                    h_scr, hperm_scr, acc_scr):
    j = pl.program_id(1)
    kind = tab_ref[_TAB_KIND, j]
    rows = 256
    d1, d2 = ATT_GROUPS[1][1], ATT_GROUPS[2][1]
    pblk = d2 * d2

    @pl.when(j == 0)
    def _():
        def body(c, carry):
            r = pl.ds(pl.multiple_of(c * rows, rows), rows)
            h_scr[r, :] = _layer_norm(x_ref[r, :], g_ref[...], b_ref[...]).astype(BF16)
            return carry
        lax.fori_loop(0, PROJ_TM // rows, body, 0)

        ri = lax.broadcasted_iota(jnp.int32, (pblk, pblk), 0)
        ci = lax.broadcasted_iota(jnp.int32, (pblk, pblk), 1)
        perm = (ci == (ri % d2) * d2 + ri // d2).astype(BF16)

        def permute(c, carry):
            r = pl.ds(pl.multiple_of(c * pblk, pblk), pblk)
            hperm_scr[r, :] = _dot(perm, h_scr[r, :]).astype(BF16)
            return carry
        lax.fori_loop(0, PROJ_TM // pblk, permute, 0)

    @pl.when(kind == _KINDS.index("A"))
    def _():
        a_ref[...] = _dot(h_scr[...], w_ref[...]).astype(BF16)

    @pl.when(kind == _KINDS.index("B"))
    def _():
        acc = _dot(h_scr[...], w_ref[...])
        n = PROJ_TM // d1
        for c in range(PROJ_TN // LANES):
            cs = slice(c * LANES, (c + 1) * LANES)
            acc_scr[c] = acc[:, cs]
            for r in range(d1):
                b_out_ref[r, :, cs] = acc_scr[c, pl.ds(r, n, stride=d1), :].astype(BF16)

    @pl.when(kind == _KINDS.index("C"))
    def _():
        acc = _dot(hperm_scr[...], w_ref[...]).astype(BF16)
        for blk in range(PROJ_TM // pblk):
            for r in range(d2):
                src = blk * pblk + r * d2
                c_out_ref[r, blk * d2:(blk + 1) * d2, :] = acc[src:src + d2, :]

    @pl.when(kind == _KINDS.index("D"))
    def _():
        d_ref[...] = _dot(h_scr[...], w_ref[...])


def _ln_proj(x2, g, b, w_in, batch, seq):
    t = x2.shape[0]
    tpb = seq // PROJ_TM
    d1, d2 = ATT_GROUPS[1][1], ATT_GROUPS[2][1]

    def block(tab, kind, j):
        return tab[_TAB_BLOCK + _KINDS.index(kind), j]

    def stream_spec(dil, kind):
        return pl.BlockSpec((None, dil, PROJ_TM // dil, PROJ_TN),
                            lambda i, j, tab: (i // tpb, 0, i % tpb, block(tab, kind, j)))

    grid_spec = pltpu.PrefetchScalarGridSpec(
        num_scalar_prefetch=1,
        grid=(t // PROJ_TM, _NJ),
        in_specs=[
            pl.BlockSpec((PROJ_TM, D_MODEL), lambda i, j, tab: (i, 0)),
            pl.BlockSpec((1, D_MODEL), lambda i, j, tab: (0, 0)),
            pl.BlockSpec((1, D_MODEL), lambda i, j, tab: (0, 0)),
            pl.BlockSpec((D_MODEL, PROJ_TN), lambda i, j, tab: (0, tab[_TAB_WCOL, j])),
        ],
        out_specs=[
            pl.BlockSpec((PROJ_TM, PROJ_TN), lambda i, j, tab: (i, block(tab, "A", j))),
            stream_spec(d1, "B"),
            stream_spec(d2, "C"),
            pl.BlockSpec((PROJ_TM, PROJ_TN), lambda i, j, tab: (i, block(tab, "D", j))),
        ],
        scratch_shapes=[pltpu.VMEM((PROJ_TM, D_MODEL), BF16),
                        pltpu.VMEM((PROJ_TM, D_MODEL), BF16),
                        pltpu.VMEM((PROJ_TN // LANES, PROJ_TM, LANES), F32)],
    )
    return pl.pallas_call(
        _ln_proj_kernel,
        grid_spec=grid_spec,
        out_shape=[
            jax.ShapeDtypeStruct((t, A_COLS), BF16),
            jax.ShapeDtypeStruct((batch, d1, seq // d1, 3 * ATT_WIDTH), BF16),
            jax.ShapeDtypeStruct((batch, d2, seq // d2, 3 * ATT_WIDTH), BF16),
            jax.ShapeDtypeStruct((t, D_COLS), F32),
        ],
        compiler_params=pltpu.CompilerParams(
            dimension_semantics=("arbitrary", "arbitrary"), vmem_limit_bytes=VMEM_LIMIT),
        name="ln_proj",
    )(jnp.asarray(_proj_tables()), x2, g, b, w_in)


def _attn_bias_tables(slopes, dil):
    ql = jnp.arange(ATT_BQ)[:, None]
    kl = jnp.arange(ATT_BK)[None, :]
    tabs = []
    for off in (0, -ATT_HALF, -2 * ATT_HALF):
        rel = kl - ql + off
        valid = jnp.abs(rel) <= ATT_HALF
        dist = (dil * jnp.abs(rel)).astype(F32)
        bias = -slopes.astype(F32)[:, None, None] * dist[None]
        tabs.append(jnp.where(valid[None], bias, NEG_INF))
    return jnp.stack(tabs)


def _attn_kernel(q_ref, k_ref, v_ref, tab_ref, o_ref, lse_ref, *, m, tq):
    qi = pl.program_id(2)
    nblk = m // ATT_BQ
    lane = lax.broadcasted_iota(jnp.int32, (1, LANES), 1)
    lo = lane < ATT_HEAD_DIM
    scale = ATT_HEAD_DIM ** -0.5
    qmask = (jnp.where(lo, scale, 0.0).astype(BF16), jnp.where(lo, 0.0, scale).astype(BF16))

    def body(i, carry):
        blk = qi * (tq // ATT_BQ) + i
        q0 = pl.multiple_of(i * ATT_BQ, ATT_BQ)
        k0 = pl.multiple_of(jnp.clip(blk * ATT_BQ - ATT_HALF, 0, m - ATT_BK), ATT_HALF)
        var = jnp.where(blk == 0, 0, jnp.where(blk == nblk - 1, 2, 1))
        rq = pl.ds(q0, ATT_BQ)
        rk = pl.ds(k0, ATT_BK)
        for pair in range(ATT_HEADS_PER_GROUP // 2):
            cs = slice(pair * LANES, (pair + 1) * LANES)
            q2 = q_ref[rq, cs]
            k2 = k_ref[rk, cs]
            v2 = v_ref[rk, cs]
            res = []
            for sub in range(2):
                s = _dot_nt(q2 * qmask[sub], k2) + tab_ref[var, 2 * pair + sub]
                mx = jnp.max(s, axis=-1, keepdims=True)
                p = jnp.exp(s - mx)
                l = jnp.sum(p, axis=-1, keepdims=True)
                o = _dot(p.astype(BF16), v2) / l
                res.append((o, mx + jnp.log(l)))
            o_ref[rq, cs] = jnp.where(lo, res[0][0], res[1][0])
            lse_ref[rq, cs] = jnp.where(lo, res[0][1], res[1][1])
        return carry

    lax.fori_loop(0, tq // ATT_BQ, body, 0)


def _attn_group(qkv4, tab, g, dil, batch, seq):
    m = seq // dil
    tq = min(m, 1024)
    q_spec = pl.BlockSpec((None, None, tq, ATT_WIDTH), lambda b, r, i: (b, r, i, 0))
    k_spec = pl.BlockSpec((None, None, m, ATT_WIDTH), lambda b, r, i: (b, r, 0, 1))
    v_spec = pl.BlockSpec((None, None, m, ATT_WIDTH), lambda b, r, i: (b, r, 0, 2))
    tab_spec = pl.BlockSpec(tab.shape, lambda b, r, i: (0, 0, 0, 0))
    out_shape = jax.ShapeDtypeStruct((batch, dil, m, ATT_WIDTH), F32)
    return pl.pallas_call(
        functools.partial(_attn_kernel, m=m, tq=tq),
        grid=(batch, dil, m // tq),
        in_specs=[q_spec, k_spec, v_spec, tab_spec],
        out_specs=[q_spec, q_spec],
        out_shape=[out_shape, out_shape],
        compiler_params=pltpu.CompilerParams(
            dimension_semantics=("arbitrary", "arbitrary", "arbitrary"),
            vmem_limit_bytes=VMEM_LIMIT),
        name=f"dilated_attn_g{g}",
    )(qkv4, qkv4, qkv4, tab)


def _dilated_attention(qkv_groups, batch, seq):
    slopes = 2.0 ** (-ALIBI_MAX_EXP * jnp.arange(1, N_ATT_HEADS + 1, dtype=F32) / N_ATT_HEADS)
    slopes = slopes.reshape(N_GROUPS, ATT_HEADS_PER_GROUP)
    outs = []
    for g, (window, dil) in enumerate(ATT_GROUPS):
        assert window // (2 * dil) == ATT_HALF
        outs.append(_attn_group(qkv_groups[g], _attn_bias_tables(slopes[g], dil), g, dil, batch, seq))
    return outs


def _split3(x):
    hi = x.astype(BF16)
    r1 = x - hi.astype(F32)
    mid = r1.astype(BF16)
    lo = (r1 - mid.astype(F32)).astype(BF16)
    return hi, mid, lo


def _hgrn_kernel(hq_ref, hf_ref, hb_ref, hi_ref, hg_ref, lbp_ref, ng_ref, out_ref,
                 qd_scr, ki_scr, st_scr, carry_scr, *, seq):
    c = HGRN_CHUNK
    grp = HGRN_GROUP
    gr = grp * c
    n_groups = seq // gr
    row = lax.broadcasted_iota(jnp.int32, (c, c), 0)
    col = lax.broadcasted_iota(jnp.int32, (c, c), 1)
    tri = (row >= col).astype(BF16)
    tri3 = jnp.concatenate([tri, tri, tri], axis=1)
    qscale = HGRN_KEY ** -0.5

    def lower_bound(d):
        a = lbp_ref[d]
        mx = jnp.max(a, axis=0, keepdims=True)
        e = jnp.exp(a - mx)
        return e[0:1] / jnp.sum(e, axis=0, keepdims=True)

    lbs = (lower_bound(0), lower_bound(1))
    carry_scr[...] = jnp.zeros_like(carry_scr)

    def chunk_rows(x, g):
        return x[g * c:(g + 1) * c]

    def direction(d, xq, x, v, st):
        lb = lbs[d]
        q = xq * jax.nn.sigmoid(xq) * qscale
        f = lb + (1.0 - lb) * jax.nn.sigmoid(x)
        kk = 1.0 - f
        lf = jnp.log(f)
        qd, ki, kd, dec = [], [], [], []
        for g in range(grp):
            lfg, kkg = chunk_rows(lf, g), chunk_rows(kk, g)
            hi, mid, lo = _split3(lfg)
            pre = _dot(tri3, jnp.concatenate([hi, mid, lo], axis=0))
            tot = pre[c - 1:c]
            if d == 0:
                cum, cl = pre, tot
            else:
                cum = tot - pre + lfg
                cl = cum[0:1]
            ec, ecl = jnp.exp(cum), jnp.exp(cl)
            kinv = kkg / ec
            qd.append((chunk_rows(q, g) * ec).astype(BF16))
            ki.append(kinv.astype(BF16))
            kd.append((kinv * ecl).astype(BF16))
            dec.append(ecl)
        entry = [None] * grp
        for g in (range(grp) if d == 0 else reversed(range(grp))):
            entry[g] = st.astype(BF16)
            st = st * dec[g] + _dot_tn(chunk_rows(v, g), kd[g])
        return jnp.concatenate(qd, axis=0), jnp.concatenate(ki, axis=0), jnp.stack(entry), st

    def scan_body(n, carry):
        nb = n_groups - 1 - n
        rf = pl.ds(pl.multiple_of(n * gr, gr), gr)
        rb = pl.ds(pl.multiple_of(nb * gr, gr), gr)
        ins = ((hq_ref[rf, :], hf_ref[rf, :], hi_ref[rf, :], carry_scr[0]),
               (hq_ref[rb, :], hb_ref[rb, :], hi_ref[rb, :], carry_scr[1]))
        outs = [direction(d, *ins[d]) for d in range(2)]
        for d, r, ng in ((0, rf, n), (1, rb, nb)):
            qd, ki, entry, st = outs[d]
            qd_scr[d, r, :] = qd
            ki_scr[d, r, :] = ki
            st_scr[d, pl.ds(ng * grp, grp)] = entry
            carry_scr[d] = st
        return carry

    lax.fori_loop(0, n_groups, scan_body, 0)

    def out_body(n, carry):
        r = pl.ds(pl.multiple_of(n * gr, gr), gr)
        v, qf, qb = hi_ref[r, :], qd_scr[0, r, :], qd_scr[1, r, :]
        kf, kb = ki_scr[0, r, :], ki_scr[1, r, :]
        sf, sb = st_scr[0, pl.ds(n * grp, grp)], st_scr[1, pl.ds(n * grp, grp)]
        xg = hg_ref[r, :]
        first = []
        for g in range(grp):
            qfg, qbg = chunk_rows(qf, g), chunk_rows(qb, g)
            af = _dot_nt(qfg, chunk_rows(kf, g))
            ab = _dot_nt(qbg, chunk_rows(kb, g))
            inter = _dot_nt(jnp.concatenate([qfg, qbg], axis=1), jnp.concatenate([sf[g], sb[g]], axis=1))
            first.append((af, ab, inter))
        os_ = []
        for g, (af, ab, inter) in enumerate(first):
            a = jnp.where(row >= col, af, 0.0) + jnp.where(row <= col, ab, 0.0)
            os_.append(_dot(a.astype(BF16), chunk_rows(v, g)) + inter)
        o = jnp.concatenate(os_, axis=0)
        o = o * lax.rsqrt(jnp.mean(o * o, axis=-1, keepdims=True) + RMS_EPS) * ng_ref[...]
        out_ref[r, :] = (o * (xg * jax.nn.sigmoid(xg))).astype(BF16)
        return carry

    lax.fori_loop(0, n_groups, out_body, 0)


def _hgrn(proj_a, proj_d, hgrn_lb, norm_g, batch, seq):
    a3 = proj_a.reshape(batch, seq, A_COLS)
    d3 = proj_d.reshape(batch, seq, D_COLS)
    nh = HGRN_HEADS
    hi_col0 = QKV_COLS // N_GROUPS // LANES

    def col(k):
        return pl.BlockSpec((None, seq, LANES), lambda b, h: (b, 0, k * nh + h))

    n_chunks = seq // HGRN_CHUNK
    out = pl.pallas_call(
        functools.partial(_hgrn_kernel, seq=seq),
        grid=(batch, nh),
        in_specs=[
            col(0), col(1), col(2),
            pl.BlockSpec((None, seq, LANES), lambda b, h: (b, 0, hi_col0 + h)),
            col(3),
            pl.BlockSpec((2, DEPTH + 1, LANES), lambda b, h: (0, 0, h)),
            pl.BlockSpec((1, HGRN_VAL), lambda b, h: (0, 0)),
        ],
        out_specs=pl.BlockSpec((None, seq, LANES), lambda b, h: (b, 0, h)),
        out_shape=jax.ShapeDtypeStruct((batch, seq, D_MODEL), BF16),
        scratch_shapes=[
            pltpu.VMEM((2, seq, HGRN_KEY), BF16),
            pltpu.VMEM((2, seq, HGRN_KEY), BF16),
            pltpu.VMEM((2, n_chunks, HGRN_VAL, HGRN_KEY), BF16),
            pltpu.VMEM((2, HGRN_VAL, HGRN_KEY), F32),
        ],
        compiler_params=pltpu.CompilerParams(
            dimension_semantics=("arbitrary", "arbitrary"), vmem_limit_bytes=VMEM_LIMIT),
        name="hgrn2",
    )(d3, d3, d3, a3, d3, hgrn_lb, norm_g)
    return out.reshape(batch * seq, D_MODEL)


def _merge_kernel(x_ref, lg_ref, lb_ref, o0_ref, l0_ref, o1_ref, l1_ref, o2_ref, l2_ref, rec_ref,
                  wg_ref, wa_ref, wr_ref, wo_ref, g1_ref, b1_ref, out_ref,
                  o1_scr, l1_scr, o2_scr, l2_scr):
    tm = MERGE_TM
    for src, dst, dil in ((o1_ref, o1_scr, ATT_GROUPS[1][1]), (l1_ref, l1_scr, ATT_GROUPS[1][1]),
                          (o2_ref, o2_scr, ATT_GROUPS[2][1]), (l2_ref, l2_scr, ATT_GROUPS[2][1])):
        for r in range(dil):
            for c in range(ATT_WIDTH // LANES):
                dst[c, pl.ds(r, tm // dil, stride=dil), :] = src[r, :, c * LANES:(c + 1) * LANES]

    def token_order(scr, r):
        return jnp.concatenate([scr[c, r, :] for c in range(ATT_WIDTH // LANES)], axis=1)

    def attention(r):
        l0, l1, l2 = l0_ref[r, :], token_order(l1_scr, r), token_order(l2_scr, r)
        mx = jnp.maximum(jnp.maximum(l0, l1), l2)
        e0, e1, e2 = jnp.exp(l0 - mx), jnp.exp(l1 - mx), jnp.exp(l2 - mx)
        return (e0 * o0_ref[r, :] + e1 * token_order(o1_scr, r) + e2 * token_order(o2_scr, r)) / (e0 + e1 + e2)

    rs = tm // MERGE_SUB
    subs = [slice(s * rs, (s + 1) * rs) for s in range(MERGE_SUB)]
    ur = [_dot(rec_ref[r, :], wr_ref[...]) for r in subs]
    h = [_layer_norm(x_ref[r, :], lg_ref[...], lb_ref[...]) for r in subs]
    gates = [_dot(hs.astype(BF16), wg_ref[...]) for hs in h]
    ua = [_dot(attention(r).astype(BF16), wa_ref[...]) for r in subs]
    merged = [jax.nn.sigmoid(g[:, :D_MODEL]) * a + jax.nn.sigmoid(g[:, D_MODEL:]) * u
              for g, a, u in zip(gates, ua, ur)]
    z = [_dot(m.astype(BF16), wo_ref[...]) for m in merged]
    for r, hs, zs in zip(subs, h, z):
        out_ref[r, :] = _layer_norm(DEEPNORM_ALPHA * hs + zs, g1_ref[...], b1_ref[...])


def _const_spec(shape):
    return pl.BlockSpec(shape, lambda i: (0,) * len(shape))


def _merge(x2, lg, lb, att_groups, rec, wg, wa, wr, wo, g1, b1, batch, seq):
    t = x2.shape[0]
    tm = MERGE_TM
    tpb = seq // tm
    att_args, att_specs = [], []
    for (o, lse), (_, dil) in zip(att_groups, ATT_GROUPS):
        if dil == 1:
            spec = pl.BlockSpec((tm, ATT_WIDTH), lambda i: (i, 0))
            o, lse = o.reshape(t, ATT_WIDTH), lse.reshape(t, ATT_WIDTH)
        else:
            spec = pl.BlockSpec((None, dil, tm // dil, ATT_WIDTH), lambda i: (i // tpb, 0, i % tpb, 0))
        att_args += [o, lse]
        att_specs += [spec, spec]
    return pl.pallas_call(
        _merge_kernel,
        grid=(t // tm,),
        in_specs=[
            pl.BlockSpec((tm, D_MODEL), lambda i: (i, 0)),
            _const_spec((1, D_MODEL)), _const_spec((1, D_MODEL)),
            *att_specs,
            pl.BlockSpec((tm, D_MODEL), lambda i: (i, 0)),
            _const_spec(wg.shape), _const_spec(wa.shape), _const_spec(wr.shape), _const_spec(wo.shape),
            _const_spec((1, D_MODEL)), _const_spec((1, D_MODEL)),
        ],
        out_specs=pl.BlockSpec((tm, D_MODEL), lambda i: (i, 0)),
        out_shape=jax.ShapeDtypeStruct((t, D_MODEL), F32),
        scratch_shapes=[pltpu.VMEM((ATT_WIDTH // LANES, tm, LANES), F32)] * 4,
        compiler_params=pltpu.CompilerParams(
            dimension_semantics=("arbitrary",), vmem_limit_bytes=VMEM_LIMIT),
        name="merge_ln1",
    )(x2, lg, lb, *att_args, rec, wg, wa, wr, wo, g1, b1)


def _ffn_kernel(h_ref, wi_ref, wo_ref, g2_ref, b2_ref, out_ref, a_scr):
    h = h_ref[...]
    hb = h.astype(BF16)
    for c in range(D_FF // FFN_TC):
        gate = _dot(hb, wi_ref[:, c * FFN_TC:(c + 1) * FFN_TC])
        up = _dot(hb, wi_ref[:, D_FF + c * FFN_TC:D_FF + (c + 1) * FFN_TC])
        a_scr[:, c * FFN_TC:(c + 1) * FFN_TC] = (gate * jax.nn.sigmoid(gate) * up).astype(BF16)
    y = DEEPNORM_ALPHA * h + _dot(a_scr[...], wo_ref[...])
    out_ref[...] = _layer_norm(y, g2_ref[...], b2_ref[...])


def _ffn(h1, wi, wo, g2, b2):
    t = h1.shape[0]
    tm = FFN_TM
    return pl.pallas_call(
        _ffn_kernel,
        grid=(t // tm,),
        in_specs=[
            pl.BlockSpec((tm, D_MODEL), lambda i: (i, 0)),
            _const_spec(wi.shape), _const_spec(wo.shape),
            _const_spec((1, D_MODEL)), _const_spec((1, D_MODEL)),
        ],
        out_specs=pl.BlockSpec((tm, D_MODEL), lambda i: (i, 0)),
        out_shape=jax.ShapeDtypeStruct((t, D_MODEL), F32),
        scratch_shapes=[pltpu.VMEM((tm, D_FF), BF16)],
        compiler_params=pltpu.CompilerParams(
            dimension_semantics=("arbitrary",), vmem_limit_bytes=VMEM_LIMIT),
        name="ffn_ln2",
    )(h1, wi, wo, g2, b2)


def kernel(x, ln_in_g, ln_in_b, w_in, hgrn_lb, hgrn_norm_g, w_att_up, w_hgrn_up, w_o,
           ln1_g, ln1_b, w_ffn_in, w_ffn_out, ln2_g, ln2_b):
    batch, seq, d = x.shape
    assert d == D_MODEL and w_in.shape == (DEPTH, D_MODEL, IN_COLS)
    assert seq % PROJ_TM == 0 and seq % (16 * ATT_BK) == 0
    x2 = x.reshape(batch * seq, d)
    row = lambda v: v.reshape(1, -1).astype(F32)
    lg, lb = row(ln_in_g), row(ln_in_b)
    w_in_b = w_in[0].astype(BF16)

    proj_a, proj_b, proj_c, proj_d = _ln_proj(x2, lg, lb, w_in_b, batch, seq)
    att_groups = _dilated_attention(
        (proj_a.reshape(batch, 1, seq, A_COLS), proj_b, proj_c), batch, seq)
    rec = _hgrn(proj_a, proj_d, hgrn_lb.astype(F32), row(hgrn_norm_g[0]), batch, seq)
    h1 = _merge(x2, lg, lb, att_groups, rec, w_in_b[:, GATE_COL0:],
                w_att_up[0].astype(BF16), w_hgrn_up[0].astype(BF16), w_o[0].astype(BF16),
                row(ln1_g[0]), row(ln1_b[0]), batch, seq)
    out = _ffn(h1, w_ffn_in[0].astype(BF16), w_ffn_out[0].astype(BF16), row(ln2_g[0]), row(ln2_b[0]))
    return out.reshape(batch, seq, d)
```

```python
import functools

import jax
import jax.numpy as jnp
import numpy as np
from jax import lax
from jax.experimental import pallas as pl
from jax.experimental.pallas import tpu as pltpu

F32 = jnp.float32
BF16 = jnp.bfloat16

D_MODEL = 1024
DEPTH = 1
ATT_GROUPS = ((128, 1), (512, 4), (2048, 16))
N_GROUPS = len(ATT_GROUPS)
ATT_HEADS_PER_GROUP = 8
ATT_HEAD_DIM = 64
N_ATT_HEADS = N_GROUPS * ATT_HEADS_PER_GROUP
ATT_QKV = N_ATT_HEADS * ATT_HEAD_DIM
ATT_WIDTH = ATT_HEADS_PER_GROUP * ATT_HEAD_DIM
ALIBI_MAX_EXP = 8.0
NEG_INF = -1e30
HGRN_HEADS = 8
HGRN_KEY = 128
HGRN_VAL = 128
HGRN_CHUNK = 64
D_FF = 2816
DEEPNORM_ALPHA = (2.0 * DEPTH) ** 0.25
LN_EPS = 1e-5
RMS_EPS = 1e-6
QKV_COLS = 3 * ATT_QKV
REC_COLS = 3 * D_MODEL
GATE_COLS = 3 * D_MODEL
IN_COLS = QKV_COLS + REC_COLS + D_MODEL + GATE_COLS

LANES = 128
VMEM_LIMIT = 56 * 1024 * 1024
PROJ_TM = 2048
PROJ_TN = 512
ATT_BQ = 128
ATT_BK = 256
ATT_HALF = 64
HGRN_GROUP = 8
MERGE_TM = 512
MERGE_SUB = 2
FFN_TM = 512
FFN_TC = 256


def _layer_norm(x, g, b):
    mu = jnp.mean(x, axis=-1, keepdims=True)
    xc = x - mu
    var = jnp.mean(xc * xc, axis=-1, keepdims=True)
    return xc * lax.rsqrt(var + LN_EPS) * g + b


def _dot(a, b):
    return jnp.dot(a, b, preferred_element_type=F32)


def _dot_nt(a, b):
    return lax.dot_general(a, b, (((1,), (1,)), ((), ())), preferred_element_type=F32)


def _dot_tn(a, b):
    return lax.dot_general(a, b, (((0,), (0,)), ((), ())), preferred_element_type=F32)


_PROJ_STEPS = (("D", 9), ("A", 0), ("D", 10), ("A", 3), ("D", 11), ("A", 6), ("D", 12), ("A", 15),
               ("D", 13), ("A", 16), ("D", 14), ("B", 1), ("D", 17), ("B", 4), ("D", 18), ("B", 7),
               ("C", 2), ("C", 5), ("C", 8))
_NJ = len(_PROJ_STEPS)
_KINDS = "ABCD"
_N_TILES = {kind: sum(k == kind for k, _ in _PROJ_STEPS) for kind in "ABCD"}
A_HI_TILE = 3
GATE_COL0 = QKV_COLS + REC_COLS + 2 * D_MODEL
_TAB_WCOL, _TAB_KIND, _TAB_BLOCK = 0, 1, 2


def _proj_tables():
    rows = [[w for _, w in _PROJ_STEPS], [_KINDS.index(k) for k, _ in _PROJ_STEPS]]
    for kind in _KINDS:
        n, blocks = 0, []
        for k, _ in _PROJ_STEPS:
            n += k == kind
            blocks.append(max(n - 1, 0))
        rows.append(blocks)
    return np.asarray(rows, np.int32)


def _ln_proj_kernel(tab_ref, x_ref, g_ref, b_ref, w_ref, a_ref, b_out_ref, c_out_ref, d_ref,
                    h_scr, hperm_scr, acc_scr):
    j = pl.program_id(1)
    kind = tab_ref[_TAB_KIND, j]
    rows = 256
    d1, d2 = ATT_GROUPS[1][1], ATT_GROUPS[2][1]
    pblk = d2 * d2

    @pl.when(j == 0)
    def _():
        def body(c, carry):
            r = pl.ds(pl.multiple_of(c * rows, rows), rows)
            h_scr[r, :] = _layer_norm(x_ref[r, :], g_ref[...], b_ref[...]).astype(BF16)
            return carry
        lax.fori_loop(0, PROJ_TM // rows, body, 0)

        ri = lax.broadcasted_iota(jnp.int32, (pblk, pblk), 0)
        ci = lax.broadcasted_iota(jnp.int32, (pblk, pblk), 1)
        perm = (ci == (ri % d2) * d2 + ri // d2).astype(BF16)

        def permute(c, carry):
            r = pl.ds(pl.multiple_of(c * pblk, pblk), pblk)
            hperm_scr[r, :] = _dot(perm, h_scr[r, :]).astype(BF16)
            return carry
        lax.fori_loop(0, PROJ_TM // pblk, permute, 0)

    @pl.when(kind == _KINDS.index("A"))
    def _():
        a_ref[...] = _dot(h_scr[...], w_ref[...]).astype(BF16)

    @pl.when(kind == _KINDS.index("B"))
    def _():
        acc = _dot(h_scr[...], w_ref[...])
        n = PROJ_TM // d1
        for c in range(PROJ_TN // LANES):
            cs = slice(c * LANES, (c + 1) * LANES)
            acc_scr[c] = acc[:, cs]
            for r in range(d1):
                b_out_ref[r, :, cs] = acc_scr[c, pl.ds(r, n, stride=d1), :].astype(BF16)

    @pl.when(kind == _KINDS.index("C"))
    def _():
        acc = _dot(hperm_scr[...], w_ref[...]).astype(BF16)
        for blk in range(PROJ_TM // pblk):
            for r in range(d2):
                src = blk * pblk + r * d2
                c_out_ref[r, blk * d2:(blk + 1) * d2, :] = acc[src:src + d2, :]

    @pl.when(kind == _KINDS.index("D"))
    def _():
        d_ref[...] = _dot(h_scr[...], w_ref[...])


def _ln_proj(x2, g, b, w_in, batch, seq):
    t = x2.shape[0]
    tpb = seq // PROJ_TM
    d1, d2 = ATT_GROUPS[1][1], ATT_GROUPS[2][1]

    def block(tab, kind, j):
        return tab[_TAB_BLOCK + _KINDS.index(kind), j]

    def token_spec(kind):
        return pl.BlockSpec((None, PROJ_TM, PROJ_TN), lambda i, j, tab: (block(tab, kind, j), i, 0))

    def stream_spec(dil, kind):
        return pl.BlockSpec((None, None, dil, PROJ_TM // dil, PROJ_TN),
                            lambda i, j, tab: (block(tab, kind, j), i // tpb, 0, i % tpb, 0))

    grid_spec = pltpu.PrefetchScalarGridSpec(
        num_scalar_prefetch=1,
        grid=(t // PROJ_TM, _NJ),
        in_specs=[
            pl.BlockSpec((PROJ_TM, D_MODEL), lambda i, j, tab: (i, 0)),
            pl.BlockSpec((1, D_MODEL), lambda i, j, tab: (0, 0)),
            pl.BlockSpec((1, D_MODEL), lambda i, j, tab: (0, 0)),
            pl.BlockSpec((D_MODEL, PROJ_TN), lambda i, j, tab: (0, tab[_TAB_WCOL, j])),
        ],
        out_specs=[
            token_spec("A"),
            stream_spec(d1, "B"),
            stream_spec(d2, "C"),
            token_spec("D"),
        ],
        scratch_shapes=[pltpu.VMEM((PROJ_TM, D_MODEL), BF16),
                        pltpu.VMEM((PROJ_TM, D_MODEL), BF16),
                        pltpu.VMEM((PROJ_TN // LANES, PROJ_TM, LANES), F32)],
    )
    return pl.pallas_call(
        _ln_proj_kernel,
        grid_spec=grid_spec,
        out_shape=[
            jax.ShapeDtypeStruct((_N_TILES["A"], t, PROJ_TN), BF16),
            jax.ShapeDtypeStruct((_N_TILES["B"], batch, d1, seq // d1, PROJ_TN), BF16),
            jax.ShapeDtypeStruct((_N_TILES["C"], batch, d2, seq // d2, PROJ_TN), BF16),
            jax.ShapeDtypeStruct((_N_TILES["D"], t, PROJ_TN), F32),
        ],
        compiler_params=pltpu.CompilerParams(
            dimension_semantics=("arbitrary", "arbitrary"), vmem_limit_bytes=VMEM_LIMIT),
        name="ln_proj",
    )(jnp.asarray(_proj_tables()), x2, g, b, w_in)


def _attn_bias_tables(slopes, dil):
    ql = jnp.arange(ATT_BQ)[:, None]
    kl = jnp.arange(ATT_BK)[None, :]
    tabs = []
    for off in (0, -ATT_HALF, -2 * ATT_HALF):
        rel = kl - ql + off
        valid = jnp.abs(rel) <= ATT_HALF
        dist = (dil * jnp.abs(rel)).astype(F32)
        bias = -slopes.astype(F32)[:, None, None] * dist[None]
        tabs.append(jnp.where(valid[None], bias, NEG_INF))
    return jnp.stack(tabs)


def _attn_kernel(q_ref, k_ref, v_ref, tab_ref, o_ref, lse_ref, *, m, tq):
    qi = pl.program_id(2)
    nblk = m // ATT_BQ
    lane = lax.broadcasted_iota(jnp.int32, (1, LANES), 1)
    lo = lane < ATT_HEAD_DIM
    scale = ATT_HEAD_DIM ** -0.5
    qmask = (jnp.where(lo, scale, 0.0).astype(BF16), jnp.where(lo, 0.0, scale).astype(BF16))

    def body(i, carry):
        blk = qi * (tq // ATT_BQ) + i
        q0 = pl.multiple_of(i * ATT_BQ, ATT_BQ)
        k0 = pl.multiple_of(jnp.clip(blk * ATT_BQ - ATT_HALF, 0, m - ATT_BK), ATT_HALF)
        var = jnp.where(blk == 0, 0, jnp.where(blk == nblk - 1, 2, 1))
        rq = pl.ds(q0, ATT_BQ)
        rk = pl.ds(k0, ATT_BK)
        cols = [slice(pair * LANES, (pair + 1) * LANES) for pair in range(ATT_HEADS_PER_GROUP // 2)]
        scores = []
        for pair, cs in enumerate(cols):
            q2 = q_ref[rq, cs]
            k2 = k_ref[rk, cs]
            for sub in range(2):
                scores.append(_dot_nt(q2 * qmask[sub], k2) + tab_ref[var, 2 * pair + sub])
        probs = []
        for s in scores:
            mx = jnp.max(s, axis=-1, keepdims=True)
            p = jnp.exp(s - mx)
            l = jnp.sum(p, axis=-1, keepdims=True)
            probs.append((p.astype(BF16), l, mx + jnp.log(l)))
        for pair, cs in enumerate(cols):
            v2 = v_ref[rk, cs]
            (p0, l0, lse0), (p1, l1, lse1) = probs[2 * pair], probs[2 * pair + 1]
            o_ref[rq, cs] = jnp.where(lo, _dot(p0, v2) / l0, _dot(p1, v2) / l1)
            lse_ref[rq, cs] = jnp.where(lo, lse0, lse1)
        return carry

    lax.fori_loop(0, tq // ATT_BQ, body, 0, unroll=2)


def _attn_group(qkv5, tab, g, dil, batch, seq):
    m = seq // dil
    tq = min(m, 1024)
    q_spec = pl.BlockSpec((None, None, None, tq, ATT_WIDTH), lambda b, r, i: (0, b, r, i, 0))
    k_spec = pl.BlockSpec((None, None, None, m, ATT_WIDTH), lambda b, r, i: (1, b, r, 0, 0))
    v_spec = pl.BlockSpec((None, None, None, m, ATT_WIDTH), lambda b, r, i: (2, b, r, 0, 0))
    o_spec = pl.BlockSpec((None, None, tq, ATT_WIDTH), lambda b, r, i: (b, r, i, 0))
    tab_spec = pl.BlockSpec(tab.shape, lambda b, r, i: (0, 0, 0, 0))
    out_shape = jax.ShapeDtypeStruct((batch, dil, m, ATT_WIDTH), F32)
    return pl.pallas_call(
        functools.partial(_attn_kernel, m=m, tq=tq),
        grid=(batch, dil, m // tq),
        in_specs=[q_spec, k_spec, v_spec, tab_spec],
        out_specs=[o_spec, o_spec],
        out_shape=[out_shape, out_shape],
        compiler_params=pltpu.CompilerParams(
            dimension_semantics=("arbitrary", "arbitrary", "arbitrary"),
            vmem_limit_bytes=VMEM_LIMIT),
        name=f"dilated_attn_g{g}",
    )(qkv5, qkv5, qkv5, tab)


def _dilated_attention(qkv_groups, batch, seq):
    slopes = 2.0 ** (-ALIBI_MAX_EXP * jnp.arange(1, N_ATT_HEADS + 1, dtype=F32) / N_ATT_HEADS)
    slopes = slopes.reshape(N_GROUPS, ATT_HEADS_PER_GROUP)
    outs = []
    for g, (window, dil) in enumerate(ATT_GROUPS):
        assert window // (2 * dil) == ATT_HALF
        outs.append(_attn_group(qkv_groups[g], _attn_bias_tables(slopes[g], dil), g, dil, batch, seq))
    return outs


def _split3(x):
    hi = x.astype(BF16)
    r1 = x - hi.astype(F32)
    mid = r1.astype(BF16)
    lo = (r1 - mid.astype(F32)).astype(BF16)
    return hi, mid, lo


def _hgrn_kernel(hq_ref, hf_ref, hb_ref, hi_ref, hg_ref, lbp_ref, ng_ref, out_ref,
                 qd_scr, ki_scr, st_scr, carry_scr, *, seq):
    c = HGRN_CHUNK
    grp = HGRN_GROUP
    gr = grp * c
    n_groups = seq // gr
    row = lax.broadcasted_iota(jnp.int32, (c, c), 0)
    col = lax.broadcasted_iota(jnp.int32, (c, c), 1)
    tri = (row >= col).astype(BF16)
    tri3 = jnp.concatenate([tri, tri, tri], axis=1)
    qscale = HGRN_KEY ** -0.5

    def lower_bound(d):
        a = lbp_ref[d]
        mx = jnp.max(a, axis=0, keepdims=True)
        e = jnp.exp(a - mx)
        return e[0:1] / jnp.sum(e, axis=0, keepdims=True)

    lbs = (lower_bound(0), lower_bound(1))
    carry_scr[...] = jnp.zeros_like(carry_scr)

    def chunk_rows(x, g):
        return x[g * c:(g + 1) * c]

    def direction(d, xq, x, v, st):
        lb = lbs[d]
        q = xq * jax.nn.sigmoid(xq) * qscale
        f = lb + (1.0 - lb) * jax.nn.sigmoid(x)
        kk = 1.0 - f
        lf = jnp.log(f)
        qd, ki, kd, dec = [], [], [], []
        for g in range(grp):
            lfg, kkg = chunk_rows(lf, g), chunk_rows(kk, g)
            hi, mid, lo = _split3(lfg)
            pre = _dot(tri3, jnp.concatenate([hi, mid, lo], axis=0))
            tot = pre[c - 1:c]
            if d == 0:
                cum, cl = pre, tot
            else:
                cum = tot - pre + lfg
                cl = cum[0:1]
            ec, ecl = jnp.exp(cum), jnp.exp(cl)
            kinv = kkg / ec
            qd.append((chunk_rows(q, g) * ec).astype(BF16))
            ki.append(kinv.astype(BF16))
            kd.append((kinv * ecl).astype(BF16))
            dec.append(ecl)
        entry = [None] * grp
        for g in (range(grp) if d == 0 else reversed(range(grp))):
            entry[g] = st.astype(BF16)
            st = st * dec[g] + _dot_tn(chunk_rows(v, g), kd[g])
        return jnp.concatenate(qd, axis=0), jnp.concatenate(ki, axis=0), jnp.stack(entry), st

    def scan_body(n, carry):
        nb = n_groups - 1 - n
        rf = pl.ds(pl.multiple_of(n * gr, gr), gr)
        rb = pl.ds(pl.multiple_of(nb * gr, gr), gr)
        ins = ((hq_ref[rf, :], hf_ref[rf, :], hi_ref[rf, :], carry_scr[0]),
               (hq_ref[rb, :], hb_ref[rb, :], hi_ref[rb, :], carry_scr[1]))
        outs = [direction(d, *ins[d]) for d in range(2)]
        for d, r, ng in ((0, rf, n), (1, rb, nb)):
            qd, ki, entry, st = outs[d]
            qd_scr[d, r, :] = qd
            ki_scr[d, r, :] = ki
            st_scr[d, pl.ds(ng * grp, grp)] = entry
            carry_scr[d] = st
        return carry

    lax.fori_loop(0, n_groups, scan_body, 0)

    def out_body(n, carry):
        r = pl.ds(pl.multiple_of(n * gr, gr), gr)
        v, qf, qb = hi_ref[r, :], qd_scr[0, r, :], qd_scr[1, r, :]
        kf, kb = ki_scr[0, r, :], ki_scr[1, r, :]
        sf, sb = st_scr[0, pl.ds(n * grp, grp)], st_scr[1, pl.ds(n * grp, grp)]
        xg = hg_ref[r, :]
        first = []
        for g in range(grp):
            qfg, qbg = chunk_rows(qf, g), chunk_rows(qb, g)
            af = _dot_nt(qfg, chunk_rows(kf, g))
            ab = _dot_nt(qbg, chunk_rows(kb, g))
            inter = _dot_nt(jnp.concatenate([qfg, qbg], axis=1), jnp.concatenate([sf[g], sb[g]], axis=1))
            first.append((af, ab, inter))
        os_ = []
        for g, (af, ab, inter) in enumerate(first):
            a = jnp.where(row >= col, af, 0.0) + jnp.where(row <= col, ab, 0.0)
            os_.append(_dot(a.astype(BF16), chunk_rows(v, g)) + inter)
        o = jnp.concatenate(os_, axis=0)
        o = o * lax.rsqrt(jnp.mean(o * o, axis=-1, keepdims=True) + RMS_EPS) * ng_ref[...]
        out_ref[r, :] = (o * (xg * jax.nn.sigmoid(xg))).astype(BF16)
        return carry

    lax.fori_loop(0, n_groups, out_body, 0)


def _hgrn(proj_a, proj_d, hgrn_lb, norm_g, batch, seq):
    a4 = proj_a.reshape(proj_a.shape[0], batch, seq, PROJ_TN)
    d4 = proj_d.reshape(proj_d.shape[0], batch, seq, PROJ_TN)
    nh = HGRN_HEADS
    hpt = PROJ_TN // LANES
    tpf = D_MODEL // PROJ_TN

    def col(field, tile0=0):
        return pl.BlockSpec((None, None, seq, LANES),
                            lambda b, h: (tile0 + field * tpf + h // hpt, b, 0, h % hpt))

    n_chunks = seq // HGRN_CHUNK
    out = pl.pallas_call(
        functools.partial(_hgrn_kernel, seq=seq),
        grid=(batch, nh),
        in_specs=[
            col(0), col(1), col(2),
            col(0, A_HI_TILE),
            col(3),
            pl.BlockSpec((2, DEPTH + 1, LANES), lambda b, h: (0, 0, h)),
            pl.BlockSpec((1, HGRN_VAL), lambda b, h: (0, 0)),
        ],
        out_specs=pl.BlockSpec((None, seq, LANES), lambda b, h: (b, 0, h)),
        out_shape=jax.ShapeDtypeStruct((batch, seq, D_MODEL), BF16),
        scratch_shapes=[
            pltpu.VMEM((2, seq, HGRN_KEY), BF16),
            pltpu.VMEM((2, seq, HGRN_KEY), BF16),
            pltpu.VMEM((2, n_chunks, HGRN_VAL, HGRN_KEY), BF16),
            pltpu.VMEM((2, HGRN_VAL, HGRN_KEY), F32),
        ],
        compiler_params=pltpu.CompilerParams(
            dimension_semantics=("arbitrary", "arbitrary"), vmem_limit_bytes=VMEM_LIMIT),
        name="hgrn2",
    )(d4, d4, d4, a4, d4, hgrn_lb, norm_g)
    return out.reshape(batch * seq, D_MODEL)


def _merge_kernel(x_ref, lg_ref, lb_ref, o0_ref, l0_ref, o1_ref, l1_ref, o2_ref, l2_ref, rec_ref,
                  wg_ref, wa_ref, wr_ref, wo_ref, g1_ref, b1_ref, out_ref,
                  o1_scr, l1_scr, o2_scr, l2_scr):
    tm = MERGE_TM
    for src, dst, dil in ((o1_ref, o1_scr, ATT_GROUPS[1][1]), (l1_ref, l1_scr, ATT_GROUPS[1][1]),
                          (o2_ref, o2_scr, ATT_GROUPS[2][1]), (l2_ref, l2_scr, ATT_GROUPS[2][1])):
        for r in range(dil):
            for c in range(ATT_WIDTH // LANES):
                dst[c, pl.ds(r, tm // dil, stride=dil), :] = src[r, :, c * LANES:(c + 1) * LANES]

    def token_order(scr, r):
        return jnp.concatenate([scr[c, r, :] for c in range(ATT_WIDTH // LANES)], axis=1)

    def attention(r):
        l0, l1, l2 = l0_ref[r, :], token_order(l1_scr, r), token_order(l2_scr, r)
        mx = jnp.maximum(jnp.maximum(l0, l1), l2)
        e0, e1, e2 = jnp.exp(l0 - mx), jnp.exp(l1 - mx), jnp.exp(l2 - mx)
        return (e0 * o0_ref[r, :] + e1 * token_order(o1_scr, r) + e2 * token_order(o2_scr, r)) / (e0 + e1 + e2)

    rs = tm // MERGE_SUB
    subs = [slice(s * rs, (s + 1) * rs) for s in range(MERGE_SUB)]
    ur = [_dot(rec_ref[r, :], wr_ref[...]) for r in subs]
    h = [_layer_norm(x_ref[r, :], lg_ref[...], lb_ref[...]) for r in subs]
    gates = [_dot(hs.astype(BF16), wg_ref[...]) for hs in h]
    ua = [_dot(attention(r).astype(BF16), wa_ref[...]) for r in subs]
    merged = [jax.nn.sigmoid(g[:, :D_MODEL]) * a + jax.nn.sigmoid(g[:, D_MODEL:]) * u
              for g, a, u in zip(gates, ua, ur)]
    z = [_dot(m.astype(BF16), wo_ref[...]) for m in merged]
    for r, hs, zs in zip(subs, h, z):
        out_ref[r, :] = _layer_norm(DEEPNORM_ALPHA * hs + zs, g1_ref[...], b1_ref[...])


def _const_spec(shape):
    return pl.BlockSpec(shape, lambda i: (0,) * len(shape))


def _merge(x2, lg, lb, att_groups, rec, wg, wa, wr, wo, g1, b1, batch, seq):
    t = x2.shape[0]
    tm = MERGE_TM
    tpb = seq // tm
    att_args, att_specs = [], []
    for (o, lse), (_, dil) in zip(att_groups, ATT_GROUPS):
        if dil == 1:
            spec = pl.BlockSpec((tm, ATT_WIDTH), lambda i: (i, 0))
            o, lse = o.reshape(t, ATT_WIDTH), lse.reshape(t, ATT_WIDTH)
        else:
            spec = pl.BlockSpec((None, dil, tm // dil, ATT_WIDTH), lambda i: (i // tpb, 0, i % tpb, 0))
        att_args += [o, lse]
        att_specs += [spec, spec]
    return pl.pallas_call(
        _merge_kernel,
        grid=(t // tm,),
        in_specs=[
            pl.BlockSpec((tm, D_MODEL), lambda i: (i, 0)),
            _const_spec((1, D_MODEL)), _const_spec((1, D_MODEL)),
            *att_specs,
            pl.BlockSpec((tm, D_MODEL), lambda i: (i, 0)),
            _const_spec(wg.shape), _const_spec(wa.shape), _const_spec(wr.shape), _const_spec(wo.shape),
            _const_spec((1, D_MODEL)), _const_spec((1, D_MODEL)),
        ],
        out_specs=pl.BlockSpec((tm, D_MODEL), lambda i: (i, 0)),
        out_shape=jax.ShapeDtypeStruct((t, D_MODEL), F32),
        scratch_shapes=[pltpu.VMEM((ATT_WIDTH // LANES, tm, LANES), F32)] * 4,
        compiler_params=pltpu.CompilerParams(
            dimension_semantics=("arbitrary",), vmem_limit_bytes=VMEM_LIMIT),
        name="merge_ln1",
    )(x2, lg, lb, *att_args, rec, wg, wa, wr, wo, g1, b1)


def _ffn_kernel(h_ref, wi_ref, wo_ref, g2_ref, b2_ref, out_ref, a_scr):
    h = h_ref[...]
    hb = h.astype(BF16)
    for c in range(D_FF // FFN_TC):
        gate = _dot(hb, wi_ref[:, c * FFN_TC:(c + 1) * FFN_TC])
        up = _dot(hb, wi_ref[:, D_FF + c * FFN_TC:D_FF + (c + 1) * FFN_TC])
        a_scr[:, c * FFN_TC:(c + 1) * FFN_TC] = (gate * jax.nn.sigmoid(gate) * up).astype(BF16)
    y = DEEPNORM_ALPHA * h + _dot(a_scr[...], wo_ref[...])
    out_ref[...] = _layer_norm(y, g2_ref[...], b2_ref[...])


def _ffn(h1, wi, wo, g2, b2):
    t = h1.shape[0]
    tm = FFN_TM
    return pl.pallas_call(
        _ffn_kernel,
        grid=(t // tm,),
        in_specs=[
            pl.BlockSpec((tm, D_MODEL), lambda i: (i, 0)),
            _const_spec(wi.shape), _const_spec(wo.shape),
            _const_spec((1, D_MODEL)), _const_spec((1, D_MODEL)),
        ],
        out_specs=pl.BlockSpec((tm, D_MODEL), lambda i: (i, 0)),
        out_shape=jax.ShapeDtypeStruct((t, D_MODEL), F32),
        scratch_shapes=[pltpu.VMEM((tm, D_FF), BF16)],
        compiler_params=pltpu.CompilerParams(
            dimension_semantics=("arbitrary",), vmem_limit_bytes=VMEM_LIMIT),
        name="ffn_ln2",
    )(h1, wi, wo, g2, b2)


def kernel(x, ln_in_g, ln_in_b, w_in, hgrn_lb, hgrn_norm_g, w_att_up, w_hgrn_up, w_o,
           ln1_g, ln1_b, w_ffn_in, w_ffn_out, ln2_g, ln2_b):
    batch, seq, d = x.shape
    assert d == D_MODEL and w_in.shape == (DEPTH, D_MODEL, IN_COLS)
    assert seq % PROJ_TM == 0 and seq % (16 * ATT_BK) == 0
    x2 = x.reshape(batch * seq, d)
    row = lambda v: v.reshape(1, -1).astype(F32)
    lg, lb = row(ln_in_g), row(ln_in_b)
    w_in_b = w_in[0].astype(BF16)

    proj_a, proj_b, proj_c, proj_d = _ln_proj(x2, lg, lb, w_in_b, batch, seq)
    att_groups = _dilated_attention(
        (proj_a.reshape(proj_a.shape[0], batch, 1, seq, PROJ_TN), proj_b, proj_c), batch, seq)
    rec = _hgrn(proj_a, proj_d, hgrn_lb.astype(F32), row(hgrn_norm_g[0]), batch, seq)
    h1 = _merge(x2, lg, lb, att_groups, rec, w_in_b[:, GATE_COL0:],
                w_att_up[0].astype(BF16), w_hgrn_up[0].astype(BF16), w_o[0].astype(BF16),
                row(ln1_g[0]), row(ln1_b[0]), batch, seq)
    out = _ffn(h1, w_ffn_in[0].astype(BF16), w_ffn_out[0].astype(BF16), row(ln2_g[0]), row(ln2_b[0]))
    return out.reshape(batch, seq, d)
```

```python
import functools

import jax
import jax.numpy as jnp
import numpy as np
from jax import lax
from jax.experimental import pallas as pl
from jax.experimental.pallas import tpu as pltpu

F32 = jnp.float32
BF16 = jnp.bfloat16

D_MODEL = 1024
DEPTH = 1
ATT_GROUPS = ((128, 1), (512, 4), (2048, 16))
N_GROUPS = len(ATT_GROUPS)
ATT_HEADS_PER_GROUP = 8
ATT_HEAD_DIM = 64
N_ATT_HEADS = N_GROUPS * ATT_HEADS_PER_GROUP
ATT_QKV = N_ATT_HEADS * ATT_HEAD_DIM
ATT_WIDTH = ATT_HEADS_PER_GROUP * ATT_HEAD_DIM
ALIBI_MAX_EXP = 8.0
NEG_INF = -1e30
HGRN_HEADS = 8
HGRN_KEY = 128
HGRN_VAL = 128
HGRN_CHUNK = 64
D_FF = 2816
DEEPNORM_ALPHA = (2.0 * DEPTH) ** 0.25
LN_EPS = 1e-5
RMS_EPS = 1e-6
QKV_COLS = 3 * ATT_QKV
REC_COLS = 3 * D_MODEL
GATE_COLS = 3 * D_MODEL
IN_COLS = QKV_COLS + REC_COLS + D_MODEL + GATE_COLS

LANES = 128
VMEM_LIMIT = 56 * 1024 * 1024
PROJ_TM = 2048
PROJ_TN = 512
PROJ_MM = 512
ATT_BQ = 128
ATT_BK = 256
ATT_HALF = 64
HGRN_GROUP = 8
MERGE_TM = 512
MERGE_SUB = 2
FFN_TM = 512
FFN_TC = 256


def _layer_norm(x, g, b):
    mu = jnp.mean(x, axis=-1, keepdims=True)
    xc = x - mu
    var = jnp.mean(xc * xc, axis=-1, keepdims=True)
    return xc * lax.rsqrt(var + LN_EPS) * g + b


def _dot(a, b):
    return jnp.dot(a, b, preferred_element_type=F32)


def _dot_nt(a, b):
    return lax.dot_general(a, b, (((1,), (1,)), ((), ())), preferred_element_type=F32)


def _dot_tn(a, b):
    return lax.dot_general(a, b, (((0,), (0,)), ((), ())), preferred_element_type=F32)


_PROJ_STEPS = (("D", 9), ("A", 0), ("D", 10), ("A", 3), ("D", 11), ("A", 6), ("D", 12), ("A", 15),
               ("D", 13), ("A", 16), ("D", 14), ("B", 1), ("D", 17), ("B", 4), ("D", 18), ("B", 7),
               ("C", 2), ("C", 5), ("C", 8))
_NJ = len(_PROJ_STEPS)
_KINDS = "ABCD"
_N_TILES = {kind: sum(k == kind for k, _ in _PROJ_STEPS) for kind in "ABCD"}
A_HI_TILE = 3
GATE_COL0 = QKV_COLS + REC_COLS + 2 * D_MODEL
_TAB_WCOL, _TAB_KIND, _TAB_BLOCK = 0, 1, 2


def _proj_tables():
    rows = [[w for _, w in _PROJ_STEPS], [_KINDS.index(k) for k, _ in _PROJ_STEPS]]
    for kind in _KINDS:
        n, blocks = 0, []
        for k, _ in _PROJ_STEPS:
            n += k == kind
            blocks.append(max(n - 1, 0))
        rows.append(blocks)
    return np.asarray(rows, np.int32)


def _ln_proj_kernel(tab_ref, x_ref, g_ref, b_ref, w_ref, a_ref, b_out_ref, c_out_ref, d_ref,
                    h_scr, hperm_scr, acc_scr):
    j = pl.program_id(1)
    kind = tab_ref[_TAB_KIND, j]
    rows = 256
    d1, d2 = ATT_GROUPS[1][1], ATT_GROUPS[2][1]
    pblk = d2 * d2

    @pl.when(j == 0)
    def _():
        def body(c, carry):
            r = pl.ds(pl.multiple_of(c * rows, rows), rows)
            h_scr[r, :] = _layer_norm(x_ref[r, :], g_ref[...], b_ref[...]).astype(BF16)
            return carry
        lax.fori_loop(0, PROJ_TM // rows, body, 0)

        ri = lax.broadcasted_iota(jnp.int32, (pblk, pblk), 0)
        ci = lax.broadcasted_iota(jnp.int32, (pblk, pblk), 1)
        perm = (ci == (ri % d2) * d2 + ri // d2).astype(BF16)

        def permute(c, carry):
            r = pl.ds(pl.multiple_of(c * pblk, pblk), pblk)
            hperm_scr[r, :] = _dot(perm, h_scr[r, :]).astype(BF16)
            return carry
        lax.fori_loop(0, PROJ_TM // pblk, permute, 0)

    row_chunks = [slice(c * PROJ_MM, (c + 1) * PROJ_MM) for c in range(PROJ_TM // PROJ_MM)]

    @pl.when(kind == _KINDS.index("A"))
    def _():
        for rc in row_chunks:
            a_ref[rc, :] = _dot(h_scr[rc, :], w_ref[...]).astype(BF16)

    @pl.when(kind == _KINDS.index("B"))
    def _():
        n = PROJ_TM // d1
        for rc in row_chunks:
            acc = _dot(h_scr[rc, :], w_ref[...])
            for c in range(PROJ_TN // LANES):
                acc_scr[c, rc, :] = acc[:, c * LANES:(c + 1) * LANES]
        for c in range(PROJ_TN // LANES):
            for r in range(d1):
                b_out_ref[r, :, c * LANES:(c + 1) * LANES] = (
                    acc_scr[c, pl.ds(r, n, stride=d1), :].astype(BF16))

    @pl.when(kind == _KINDS.index("C"))
    def _():
        for ci, rc in enumerate(row_chunks):
            acc = _dot(hperm_scr[rc, :], w_ref[...]).astype(BF16)
            for b in range(PROJ_MM // pblk):
                blk = ci * (PROJ_MM // pblk) + b
                for r in range(d2):
                    src = b * pblk + r * d2
                    c_out_ref[r, blk * d2:(blk + 1) * d2, :] = acc[src:src + d2, :]

    @pl.when(kind == _KINDS.index("D"))
    def _():
        for rc in row_chunks:
            d_ref[rc, :] = _dot(h_scr[rc, :], w_ref[...])


def _ln_proj(x2, g, b, w_in, batch, seq):
    t = x2.shape[0]
    tpb = seq // PROJ_TM
    d1, d2 = ATT_GROUPS[1][1], ATT_GROUPS[2][1]

    def block(tab, kind, j):
        return tab[_TAB_BLOCK + _KINDS.index(kind), j]

    def token_spec(kind):
        return pl.BlockSpec((None, PROJ_TM, PROJ_TN), lambda i, j, tab: (block(tab, kind, j), i, 0))

    def stream_spec(dil, kind):
        return pl.BlockSpec((None, None, dil, PROJ_TM // dil, PROJ_TN),
                            lambda i, j, tab: (block(tab, kind, j), i // tpb, 0, i % tpb, 0))

    grid_spec = pltpu.PrefetchScalarGridSpec(
        num_scalar_prefetch=1,
        grid=(t // PROJ_TM, _NJ),
        in_specs=[
            pl.BlockSpec((PROJ_TM, D_MODEL), lambda i, j, tab: (i, 0)),
            pl.BlockSpec((1, D_MODEL), lambda i, j, tab: (0, 0)),
            pl.BlockSpec((1, D_MODEL), lambda i, j, tab: (0, 0)),
            pl.BlockSpec((D_MODEL, PROJ_TN), lambda i, j, tab: (0, tab[_TAB_WCOL, j])),
        ],
        out_specs=[
            token_spec("A"),
            stream_spec(d1, "B"),
            stream_spec(d2, "C"),
            token_spec("D"),
        ],
        scratch_shapes=[pltpu.VMEM((PROJ_TM, D_MODEL), BF16),
                        pltpu.VMEM((PROJ_TM, D_MODEL), BF16),
                        pltpu.VMEM((PROJ_TN // LANES, PROJ_TM, LANES), F32)],
    )
    return pl.pallas_call(
        _ln_proj_kernel,
        grid_spec=grid_spec,
        out_shape=[
            jax.ShapeDtypeStruct((_N_TILES["A"], t, PROJ_TN), BF16),
            jax.ShapeDtypeStruct((_N_TILES["B"], batch, d1, seq // d1, PROJ_TN), BF16),
            jax.ShapeDtypeStruct((_N_TILES["C"], batch, d2, seq // d2, PROJ_TN), BF16),
            jax.ShapeDtypeStruct((_N_TILES["D"], t, PROJ_TN), F32),
        ],
        compiler_params=pltpu.CompilerParams(
            dimension_semantics=("arbitrary", "arbitrary"), vmem_limit_bytes=VMEM_LIMIT),
        name="ln_proj",
    )(jnp.asarray(_proj_tables()), x2, g, b, w_in)


def _attn_bias_tables(slopes, dil):
    ql = jnp.arange(ATT_BQ)[:, None]
    kl = jnp.arange(ATT_BK)[None, :]
    tabs = []
    for off in (0, -ATT_HALF, -2 * ATT_HALF):
        rel = kl - ql + off
        valid = jnp.abs(rel) <= ATT_HALF
        dist = (dil * jnp.abs(rel)).astype(F32)
        bias = -slopes.astype(F32)[:, None, None] * dist[None]
        tabs.append(jnp.where(valid[None], bias, NEG_INF))
    return jnp.stack(tabs)


def _attn_kernel(q_ref, k_ref, v_ref, tab_ref, o_ref, lse_ref, *, m, tq):
    qi = pl.program_id(2)
    nblk = m // ATT_BQ
    lane = lax.broadcasted_iota(jnp.int32, (1, LANES), 1)
    lo = lane < ATT_HEAD_DIM
    scale = ATT_HEAD_DIM ** -0.5
    qmask = (jnp.where(lo, scale, 0.0).astype(BF16), jnp.where(lo, 0.0, scale).astype(BF16))

    def body(i, carry):
        blk = qi * (tq // ATT_BQ) + i
        q0 = pl.multiple_of(i * ATT_BQ, ATT_BQ)
        k0 = pl.multiple_of(jnp.clip(blk * ATT_BQ - ATT_HALF, 0, m - ATT_BK), ATT_HALF)
        var = jnp.where(blk == 0, 0, jnp.where(blk == nblk - 1, 2, 1))
        rq = pl.ds(q0, ATT_BQ)
        rk = pl.ds(k0, ATT_BK)
        cols = [slice(pair * LANES, (pair + 1) * LANES) for pair in range(ATT_HEADS_PER_GROUP // 2)]
        scores = []
        for pair, cs in enumerate(cols):
            q2 = q_ref[rq, cs]
            k2 = k_ref[rk, cs]
            for sub in range(2):
                scores.append(_dot_nt(q2 * qmask[sub], k2) + tab_ref[var, 2 * pair + sub])
        probs = []
        for s in scores:
            mx = jnp.max(s, axis=-1, keepdims=True)
            p = jnp.exp(s - mx)
            l = jnp.sum(p, axis=-1, keepdims=True)
            probs.append((p.astype(BF16), l, mx + jnp.log(l)))
        for pair, cs in enumerate(cols):
            v2 = v_ref[rk, cs]
            (p0, l0, lse0), (p1, l1, lse1) = probs[2 * pair], probs[2 * pair + 1]
            o_ref[rq, cs] = jnp.where(lo, _dot(p0, v2) / l0, _dot(p1, v2) / l1)
            lse_ref[rq, cs] = jnp.where(lo, lse0, lse1)
        return carry

    lax.fori_loop(0, tq // ATT_BQ, body, 0, unroll=2)


def _attn_group(qkv5, tab, g, dil, batch, seq):
    m = seq // dil
    tq = min(m, 1024)
    q_spec = pl.BlockSpec((None, None, None, tq, ATT_WIDTH), lambda b, r, i: (0, b, r, i, 0))
    k_spec = pl.BlockSpec((None, None, None, m, ATT_WIDTH), lambda b, r, i: (1, b, r, 0, 0))
    v_spec = pl.BlockSpec((None, None, None, m, ATT_WIDTH), lambda b, r, i: (2, b, r, 0, 0))
    o_spec = pl.BlockSpec((None, None, tq, ATT_WIDTH), lambda b, r, i: (b, r, i, 0))
    tab_spec = pl.BlockSpec(tab.shape, lambda b, r, i: (0, 0, 0, 0))
    out_shape = jax.ShapeDtypeStruct((batch, dil, m, ATT_WIDTH), F32)
    return pl.pallas_call(
        functools.partial(_attn_kernel, m=m, tq=tq),
        grid=(batch, dil, m // tq),
        in_specs=[q_spec, k_spec, v_spec, tab_spec],
        out_specs=[o_spec, o_spec],
        out_shape=[out_shape, out_shape],
        compiler_params=pltpu.CompilerParams(
            dimension_semantics=("arbitrary", "arbitrary", "arbitrary"),
            vmem_limit_bytes=VMEM_LIMIT),
        name=f"dilated_attn_g{g}",
    )(qkv5, qkv5, qkv5, tab)


def _dilated_attention(qkv_groups, batch, seq):
    slopes = 2.0 ** (-ALIBI_MAX_EXP * jnp.arange(1, N_ATT_HEADS + 1, dtype=F32) / N_ATT_HEADS)
    slopes = slopes.reshape(N_GROUPS, ATT_HEADS_PER_GROUP)
    outs = []
    for g, (window, dil) in enumerate(ATT_GROUPS):
        assert window // (2 * dil) == ATT_HALF
        outs.append(_attn_group(qkv_groups[g], _attn_bias_tables(slopes[g], dil), g, dil, batch, seq))
    return outs


def _split3(x):
    hi = x.astype(BF16)
    r1 = x - hi.astype(F32)
    mid = r1.astype(BF16)
    lo = (r1 - mid.astype(F32)).astype(BF16)
    return hi, mid, lo


def _hgrn_kernel(hq_ref, hf_ref, hb_ref, hi_ref, hg_ref, lbp_ref, ng_ref, out_ref,
                 qd_scr, ki_scr, st_scr, carry_scr, *, seq):
    c = HGRN_CHUNK
    grp = HGRN_GROUP
    gr = grp * c
    n_groups = seq // gr
    row = lax.broadcasted_iota(jnp.int32, (c, c), 0)
    col = lax.broadcasted_iota(jnp.int32, (c, c), 1)
    tri = (row >= col).astype(BF16)
    tri3 = jnp.concatenate([tri, tri, tri], axis=1)
    qscale = HGRN_KEY ** -0.5

    def lower_bound(d):
        a = lbp_ref[d]
        mx = jnp.max(a, axis=0, keepdims=True)
        e = jnp.exp(a - mx)
        return e[0:1] / jnp.sum(e, axis=0, keepdims=True)

    lbs = (lower_bound(0), lower_bound(1))
    carry_scr[...] = jnp.zeros_like(carry_scr)

    def chunk_rows(x, g):
        return x[g * c:(g + 1) * c]

    def direction(d, xq, x, v, st):
        lb = lbs[d]
        q = xq * jax.nn.sigmoid(xq) * qscale
        f = lb + (1.0 - lb) * jax.nn.sigmoid(x)
        kk = 1.0 - f
        lf = jnp.log(f)
        qd, ki, kd, dec = [], [], [], []
        for g in range(grp):
            lfg, kkg = chunk_rows(lf, g), chunk_rows(kk, g)
            hi, mid, lo = _split3(lfg)
            pre = _dot(tri3, jnp.concatenate([hi, mid, lo], axis=0))
            tot = pre[c - 1:c]
            if d == 0:
                cum, cl = pre, tot
            else:
                cum = tot - pre + lfg
                cl = cum[0:1]
            ec, ecl = jnp.exp(cum), jnp.exp(cl)
            kinv = kkg / ec
            qd.append((chunk_rows(q, g) * ec).astype(BF16))
            ki.append(kinv.astype(BF16))
            kd.append((kinv * ecl).astype(BF16))
            dec.append(ecl)
        entry = [None] * grp
        for g in (range(grp) if d == 0 else reversed(range(grp))):
            entry[g] = st.astype(BF16)
            st = st * dec[g] + _dot_tn(chunk_rows(v, g), kd[g])
        return jnp.concatenate(qd, axis=0), jnp.concatenate(ki, axis=0), jnp.stack(entry), st

    def scan_body(n, carry):
        nb = n_groups - 1 - n
        rf = pl.ds(pl.multiple_of(n * gr, gr), gr)
        rb = pl.ds(pl.multiple_of(nb * gr, gr), gr)
        ins = ((hq_ref[rf, :], hf_ref[rf, :], hi_ref[rf, :], carry_scr[0]),
               (hq_ref[rb, :], hb_ref[rb, :], hi_ref[rb, :], carry_scr[1]))
        outs = [direction(d, *ins[d]) for d in range(2)]
        for d, r, ng in ((0, rf, n), (1, rb, nb)):
            qd, ki, entry, st = outs[d]
            qd_scr[d, r, :] = qd
            ki_scr[d, r, :] = ki
            st_scr[d, pl.ds(ng * grp, grp)] = entry
            carry_scr[d] = st
        return carry

    lax.fori_loop(0, n_groups, scan_body, 0)

    def out_body(n, carry):
        r = pl.ds(pl.multiple_of(n * gr, gr), gr)
        v, qf, qb = hi_ref[r, :], qd_scr[0, r, :], qd_scr[1, r, :]
        kf, kb = ki_scr[0, r, :], ki_scr[1, r, :]
        sf, sb = st_scr[0, pl.ds(n * grp, grp)], st_scr[1, pl.ds(n * grp, grp)]
        xg = hg_ref[r, :]
        first = []
        for g in range(grp):
            qfg, qbg = chunk_rows(qf, g), chunk_rows(qb, g)
            af = _dot_nt(qfg, chunk_rows(kf, g))
            ab = _dot_nt(qbg, chunk_rows(kb, g))
            inter = _dot_nt(jnp.concatenate([qfg, qbg], axis=1), jnp.concatenate([sf[g], sb[g]], axis=1))
            first.append((af, ab, inter))
        os_ = []
        for g, (af, ab, inter) in enumerate(first):
            a = jnp.where(row >= col, af, 0.0) + jnp.where(row <= col, ab, 0.0)
            os_.append(_dot(a.astype(BF16), chunk_rows(v, g)) + inter)
        o = jnp.concatenate(os_, axis=0)
        o = o * lax.rsqrt(jnp.mean(o * o, axis=-1, keepdims=True) + RMS_EPS) * ng_ref[...]
        out_ref[r, :] = (o * (xg * jax.nn.sigmoid(xg))).astype(BF16)
        return carry

    lax.fori_loop(0, n_groups, out_body, 0)


def _hgrn(proj_a, proj_d, hgrn_lb, norm_g, batch, seq):
    a4 = proj_a.reshape(proj_a.shape[0], batch, seq, PROJ_TN)
    d4 = proj_d.reshape(proj_d.shape[0], batch, seq, PROJ_TN)
    nh = HGRN_HEADS
    hpt = PROJ_TN // LANES
    tpf = D_MODEL // PROJ_TN

    def col(field, tile0=0):
        return pl.BlockSpec((None, None, seq, LANES),
                            lambda b, h: (tile0 + field * tpf + h // hpt, b, 0, h % hpt))

    n_chunks = seq // HGRN_CHUNK
    out = pl.pallas_call(
        functools.partial(_hgrn_kernel, seq=seq),
        grid=(batch, nh),
        in_specs=[
            col(0), col(1), col(2),
            col(0, A_HI_TILE),
            col(3),
            pl.BlockSpec((2, DEPTH + 1, LANES), lambda b, h: (0, 0, h)),
            pl.BlockSpec((1, HGRN_VAL), lambda b, h: (0, 0)),
        ],
        out_specs=pl.BlockSpec((None, seq, LANES), lambda b, h: (b, 0, h)),
        out_shape=jax.ShapeDtypeStruct((batch, seq, D_MODEL), BF16),
        scratch_shapes=[
            pltpu.VMEM((2, seq, HGRN_KEY), BF16),
            pltpu.VMEM((2, seq, HGRN_KEY), BF16),
            pltpu.VMEM((2, n_chunks, HGRN_VAL, HGRN_KEY), BF16),
            pltpu.VMEM((2, HGRN_VAL, HGRN_KEY), F32),
        ],
        compiler_params=pltpu.CompilerParams(
            dimension_semantics=("arbitrary", "arbitrary"), vmem_limit_bytes=VMEM_LIMIT),
        name="hgrn2",
    )(d4, d4, d4, a4, d4, hgrn_lb, norm_g)
    return out.reshape(batch * seq, D_MODEL)


def _merge_kernel(x_ref, lg_ref, lb_ref, o0_ref, l0_ref, o1_ref, l1_ref, o2_ref, l2_ref, rec_ref,
                  wg_ref, wa_ref, wr_ref, wo_ref, g1_ref, b1_ref, out_ref,
                  o1_scr, l1_scr, o2_scr, l2_scr):
    tm = MERGE_TM
    for src, dst, dil in ((o1_ref, o1_scr, ATT_GROUPS[1][1]), (l1_ref, l1_scr, ATT_GROUPS[1][1]),
                          (o2_ref, o2_scr, ATT_GROUPS[2][1]), (l2_ref, l2_scr, ATT_GROUPS[2][1])):
        for r in range(dil):
            for c in range(ATT_WIDTH // LANES):
                dst[c, pl.ds(r, tm // dil, stride=dil), :] = src[r, :, c * LANES:(c + 1) * LANES]

    def token_order(scr, r):
        return jnp.concatenate([scr[c, r, :] for c in range(ATT_WIDTH // LANES)], axis=1)

    def attention(r):
        l0, l1, l2 = l0_ref[r, :], token_order(l1_scr, r), token_order(l2_scr, r)
        mx = jnp.maximum(jnp.maximum(l0, l1), l2)
        e0, e1, e2 = jnp.exp(l0 - mx), jnp.exp(l1 - mx), jnp.exp(l2 - mx)
        return (e0 * o0_ref[r, :] + e1 * token_order(o1_scr, r) + e2 * token_order(o2_scr, r)) / (e0 + e1 + e2)

    rs = tm // MERGE_SUB
    subs = [slice(s * rs, (s + 1) * rs) for s in range(MERGE_SUB)]
    ur = [_dot(rec_ref[r, :], wr_ref[...]) for r in subs]
    h = [_layer_norm(x_ref[r, :], lg_ref[...], lb_ref[...]) for r in subs]
    gates = [_dot(hs.astype(BF16), wg_ref[...]) for hs in h]
    ua = [_dot(attention(r).astype(BF16), wa_ref[...]) for r in subs]
    merged = [jax.nn.sigmoid(g[:, :D_MODEL]) * a + jax.nn.sigmoid(g[:, D_MODEL:]) * u
              for g, a, u in zip(gates, ua, ur)]
    z = [_dot(m.astype(BF16), wo_ref[...]) for m in merged]
    for r, hs, zs in zip(subs, h, z):
        out_ref[r, :] = _layer_norm(DEEPNORM_ALPHA * hs + zs, g1_ref[...], b1_ref[...])


def _const_spec(shape):
    return pl.BlockSpec(shape, lambda i: (0,) * len(shape))


def _merge(x2, lg, lb, att_groups, rec, wg, wa, wr, wo, g1, b1, batch, seq):
    t = x2.shape[0]
    tm = MERGE_TM
    tpb = seq // tm
    att_args, att_specs = [], []
    for (o, lse), (_, dil) in zip(att_groups, ATT_GROUPS):
        if dil == 1:
            spec = pl.BlockSpec((tm, ATT_WIDTH), lambda i: (i, 0))
            o, lse = o.reshape(t, ATT_WIDTH), lse.reshape(t, ATT_WIDTH)
        else:
            spec = pl.BlockSpec((None, dil, tm // dil, ATT_WIDTH), lambda i: (i // tpb, 0, i % tpb, 0))
        att_args += [o, lse]
        att_specs += [spec, spec]
    return pl.pallas_call(
        _merge_kernel,
        grid=(t // tm,),
        in_specs=[
            pl.BlockSpec((tm, D_MODEL), lambda i: (i, 0)),
            _const_spec((1, D_MODEL)), _const_spec((1, D_MODEL)),
            *att_specs,
            pl.BlockSpec((tm, D_MODEL), lambda i: (i, 0)),
            _const_spec(wg.shape), _const_spec(wa.shape), _const_spec(wr.shape), _const_spec(wo.shape),
            _const_spec((1, D_MODEL)), _const_spec((1, D_MODEL)),
        ],
        out_specs=pl.BlockSpec((tm, D_MODEL), lambda i: (i, 0)),
        out_shape=jax.ShapeDtypeStruct((t, D_MODEL), F32),
        scratch_shapes=[pltpu.VMEM((ATT_WIDTH // LANES, tm, LANES), F32)] * 4,
        compiler_params=pltpu.CompilerParams(
            dimension_semantics=("arbitrary",), vmem_limit_bytes=VMEM_LIMIT),
        name="merge_ln1",
    )(x2, lg, lb, *att_args, rec, wg, wa, wr, wo, g1, b1)


def _ffn_kernel(h_ref, wi_ref, wo_ref, g2_ref, b2_ref, out_ref, a_scr):
    h = h_ref[...]
    hb = h.astype(BF16)
    for c in range(D_FF // FFN_TC):
        gate = _dot(hb, wi_ref[:, c * FFN_TC:(c + 1) * FFN_TC])
        up = _dot(hb, wi_ref[:, D_FF + c * FFN_TC:D_FF + (c + 1) * FFN_TC])
        a_scr[:, c * FFN_TC:(c + 1) * FFN_TC] = (gate * jax.nn.sigmoid(gate) * up).astype(BF16)
    y = DEEPNORM_ALPHA * h + _dot(a_scr[...], wo_ref[...])
    out_ref[...] = _layer_norm(y, g2_ref[...], b2_ref[...])


def _ffn(h1, wi, wo, g2, b2):
    t = h1.shape[0]
    tm = FFN_TM
    return pl.pallas_call(
        _ffn_kernel,
        grid=(t // tm,),
        in_specs=[
            pl.BlockSpec((tm, D_MODEL), lambda i: (i, 0)),
            _const_spec(wi.shape), _const_spec(wo.shape),
            _const_spec((1, D_MODEL)), _const_spec((1, D_MODEL)),
        ],
        out_specs=pl.BlockSpec((tm, D_MODEL), lambda i: (i, 0)),
        out_shape=jax.ShapeDtypeStruct((t, D_MODEL), F32),
        scratch_shapes=[pltpu.VMEM((tm, D_FF), BF16)],
        compiler_params=pltpu.CompilerParams(
            dimension_semantics=("arbitrary",), vmem_limit_bytes=VMEM_LIMIT),
        name="ffn_ln2",
    )(h1, wi, wo, g2, b2)


def kernel(x, ln_in_g, ln_in_b, w_in, hgrn_lb, hgrn_norm_g, w_att_up, w_hgrn_up, w_o,
           ln1_g, ln1_b, w_ffn_in, w_ffn_out, ln2_g, ln2_b):
    batch, seq, d = x.shape
    assert d == D_MODEL and w_in.shape == (DEPTH, D_MODEL, IN_COLS)
    assert seq % PROJ_TM == 0 and seq % (16 * ATT_BK) == 0
    x2 = x.reshape(batch * seq, d)
    row = lambda v: v.reshape(1, -1).astype(F32)
    lg, lb = row(ln_in_g), row(ln_in_b)
    w_in_b = w_in[0].astype(BF16)

    proj_a, proj_b, proj_c, proj_d = _ln_proj(x2, lg, lb, w_in_b, batch, seq)
    att_groups = _dilated_attention(
        (proj_a.reshape(proj_a.shape[0], batch, 1, seq, PROJ_TN), proj_b, proj_c), batch, seq)
    rec = _hgrn(proj_a, proj_d, hgrn_lb.astype(F32), row(hgrn_norm_g[0]), batch, seq)
    h1 = _merge(x2, lg, lb, att_groups, rec, w_in_b[:, GATE_COL0:],
                w_att_up[0].astype(BF16), w_hgrn_up[0].astype(BF16), w_o[0].astype(BF16),
                row(ln1_g[0]), row(ln1_b[0]), batch, seq)
    out = _ffn(h1, w_ffn_in[0].astype(BF16), w_ffn_out[0].astype(BF16), row(ln2_g[0]), row(ln2_b[0]))
    return out.reshape(batch, seq, d)
```

```python
import functools

import jax
import jax.numpy as jnp
import numpy as np
from jax import lax
from jax.experimental import pallas as pl
from jax.experimental.pallas import tpu as pltpu

F32 = jnp.float32
BF16 = jnp.bfloat16

D_MODEL = 1024
DEPTH = 1
ATT_GROUPS = ((128, 1), (512, 4), (2048, 16))
N_GROUPS = len(ATT_GROUPS)
ATT_HEADS_PER_GROUP = 8
ATT_HEAD_DIM = 64
N_ATT_HEADS = N_GROUPS * ATT_HEADS_PER_GROUP
ATT_QKV = N_ATT_HEADS * ATT_HEAD_DIM
ATT_WIDTH = ATT_HEADS_PER_GROUP * ATT_HEAD_DIM
ALIBI_MAX_EXP = 8.0
NEG_INF = -1e30
HGRN_HEADS = 8
HGRN_KEY = 128
HGRN_VAL = 128
HGRN_CHUNK = 64
D_FF = 2816
DEEPNORM_ALPHA = (2.0 * DEPTH) ** 0.25
LN_EPS = 1e-5
RMS_EPS = 1e-6
QKV_COLS = 3 * ATT_QKV
REC_COLS = 3 * D_MODEL
GATE_COLS = 3 * D_MODEL
IN_COLS = QKV_COLS + REC_COLS + D_MODEL + GATE_COLS

LANES = 128
VMEM_LIMIT = 56 * 1024 * 1024
PROJ_TM = 2048
PROJ_TN = 512
PROJ_MM = 512
ATT_BQ = 128
ATT_BK = 256
ATT_HALF = 64
HGRN_GROUP = 64
MERGE_TM = 512
MERGE_SUB = 2
FFN_TM = 512
FFN_TC = 256


def _layer_norm(x, g, b):
    mu = jnp.mean(x, axis=-1, keepdims=True)
    xc = x - mu
    var = jnp.mean(xc * xc, axis=-1, keepdims=True)
    return xc * lax.rsqrt(var + LN_EPS) * g + b


def _dot(a, b):
    return jnp.dot(a, b, preferred_element_type=F32)


def _dot_nt(a, b):
    return lax.dot_general(a, b, (((1,), (1,)), ((), ())), preferred_element_type=F32)


def _dot_tn(a, b):
    return lax.dot_general(a, b, (((0,), (0,)), ((), ())), preferred_element_type=F32)


_PROJ_STEPS = (("D", 9), ("A", 0), ("D", 10), ("A", 3), ("D", 11), ("A", 6), ("D", 12), ("A", 15),
               ("D", 13), ("A", 16), ("D", 14), ("B", 1), ("D", 17), ("B", 4), ("D", 18), ("B", 7),
               ("C", 2), ("C", 5), ("C", 8))
_NJ = len(_PROJ_STEPS)
_KINDS = "ABCD"
_N_TILES = {kind: sum(k == kind for k, _ in _PROJ_STEPS) for kind in "ABCD"}
A_HI_TILE = 3
GATE_COL0 = QKV_COLS + REC_COLS + 2 * D_MODEL
_TAB_WCOL, _TAB_KIND, _TAB_BLOCK = 0, 1, 2


def _proj_tables():
    rows = [[w for _, w in _PROJ_STEPS], [_KINDS.index(k) for k, _ in _PROJ_STEPS]]
    for kind in _KINDS:
        n, blocks = 0, []
        for k, _ in _PROJ_STEPS:
            n += k == kind
            blocks.append(max(n - 1, 0))
        rows.append(blocks)
    return np.asarray(rows, np.int32)


def _ln_proj_kernel(tab_ref, x_ref, g_ref, b_ref, w_ref, a_ref, b_out_ref, c_out_ref, d_ref,
                    h_scr, hperm_scr, acc_scr):
    j = pl.program_id(1)
    kind = tab_ref[_TAB_KIND, j]
    rows = 256
    d1, d2 = ATT_GROUPS[1][1], ATT_GROUPS[2][1]
    pblk = d2 * d2

    @pl.when(j == 0)
    def _():
        def body(c, carry):
            r = pl.ds(pl.multiple_of(c * rows, rows), rows)
            h_scr[r, :] = _layer_norm(x_ref[r, :], g_ref[...], b_ref[...]).astype(BF16)
            return carry
        lax.fori_loop(0, PROJ_TM // rows, body, 0)

        ri = lax.broadcasted_iota(jnp.int32, (pblk, pblk), 0)
        ci = lax.broadcasted_iota(jnp.int32, (pblk, pblk), 1)
        perm = (ci == (ri % d2) * d2 + ri // d2).astype(BF16)

        def permute(c, carry):
            r = pl.ds(pl.multiple_of(c * pblk, pblk), pblk)
            hperm_scr[r, :] = _dot(perm, h_scr[r, :]).astype(BF16)
            return carry
        lax.fori_loop(0, PROJ_TM // pblk, permute, 0)

    row_chunks = [slice(c * PROJ_MM, (c + 1) * PROJ_MM) for c in range(PROJ_TM // PROJ_MM)]

    @pl.when(kind == _KINDS.index("A"))
    def _():
        for rc in row_chunks:
            a_ref[rc, :] = _dot(h_scr[rc, :], w_ref[...]).astype(BF16)

    @pl.when(kind == _KINDS.index("B"))
    def _():
        n = PROJ_TM // d1
        for rc in row_chunks:
            acc = _dot(h_scr[rc, :], w_ref[...])
            for c in range(PROJ_TN // LANES):
                acc_scr[c, rc, :] = acc[:, c * LANES:(c + 1) * LANES]
        for c in range(PROJ_TN // LANES):
            for r in range(d1):
                b_out_ref[r, :, c * LANES:(c + 1) * LANES] = (
                    acc_scr[c, pl.ds(r, n, stride=d1), :].astype(BF16))

    @pl.when(kind == _KINDS.index("C"))
    def _():
        for ci, rc in enumerate(row_chunks):
            acc = _dot(hperm_scr[rc, :], w_ref[...]).astype(BF16)
            for b in range(PROJ_MM // pblk):
                blk = ci * (PROJ_MM // pblk) + b
                for r in range(d2):
                    src = b * pblk + r * d2
                    c_out_ref[r, blk * d2:(blk + 1) * d2, :] = acc[src:src + d2, :]

    @pl.when(kind == _KINDS.index("D"))
    def _():
        for rc in row_chunks:
            d_ref[rc, :] = _dot(h_scr[rc, :], w_ref[...])


def _ln_proj(x2, g, b, w_in, batch, seq):
    t = x2.shape[0]
    tpb = seq // PROJ_TM
    d1, d2 = ATT_GROUPS[1][1], ATT_GROUPS[2][1]

    def block(tab, kind, j):
        return tab[_TAB_BLOCK + _KINDS.index(kind), j]

    def token_spec(kind):
        return pl.BlockSpec((None, PROJ_TM, PROJ_TN), lambda i, j, tab: (block(tab, kind, j), i, 0))

    def stream_spec(dil, kind):
        return pl.BlockSpec((None, None, dil, PROJ_TM // dil, PROJ_TN),
                            lambda i, j, tab: (block(tab, kind, j), i // tpb, 0, i % tpb, 0))

    grid_spec = pltpu.PrefetchScalarGridSpec(
        num_scalar_prefetch=1,
        grid=(t // PROJ_TM, _NJ),
        in_specs=[
            pl.BlockSpec((PROJ_TM, D_MODEL), lambda i, j, tab: (i, 0)),
            pl.BlockSpec((1, D_MODEL), lambda i, j, tab: (0, 0)),
            pl.BlockSpec((1, D_MODEL), lambda i, j, tab: (0, 0)),
            pl.BlockSpec((D_MODEL, PROJ_TN), lambda i, j, tab: (0, tab[_TAB_WCOL, j])),
        ],
        out_specs=[
            token_spec("A"),
            stream_spec(d1, "B"),
            stream_spec(d2, "C"),
            token_spec("D"),
        ],
        scratch_shapes=[pltpu.VMEM((PROJ_TM, D_MODEL), BF16),
                        pltpu.VMEM((PROJ_TM, D_MODEL), BF16),
                        pltpu.VMEM((PROJ_TN // LANES, PROJ_TM, LANES), F32)],
    )
    return pl.pallas_call(
        _ln_proj_kernel,
        grid_spec=grid_spec,
        out_shape=[
            jax.ShapeDtypeStruct((_N_TILES["A"], t, PROJ_TN), BF16),
            jax.ShapeDtypeStruct((_N_TILES["B"], batch, d1, seq // d1, PROJ_TN), BF16),
            jax.ShapeDtypeStruct((_N_TILES["C"], batch, d2, seq // d2, PROJ_TN), BF16),
            jax.ShapeDtypeStruct((_N_TILES["D"], t, PROJ_TN), F32),
        ],
        compiler_params=pltpu.CompilerParams(
            dimension_semantics=("arbitrary", "arbitrary"), vmem_limit_bytes=VMEM_LIMIT),
        name="ln_proj",
    )(jnp.asarray(_proj_tables()), x2, g, b, w_in)


def _attn_bias_tables(slopes, dil):
    ql = jnp.arange(ATT_BQ)[:, None]
    kl = jnp.arange(ATT_BK)[None, :]
    tabs = []
    for off in (0, -ATT_HALF, -2 * ATT_HALF):
        rel = kl - ql + off
        valid = jnp.abs(rel) <= ATT_HALF
        dist = (dil * jnp.abs(rel)).astype(F32)
        bias = -slopes.astype(F32)[:, None, None] * dist[None]
        tabs.append(jnp.where(valid[None], bias, NEG_INF))
    return jnp.stack(tabs)


def _attn_kernel(q_ref, k_ref, v_ref, tab_ref, o_ref, lse_ref, *, m, tq):
    qi = pl.program_id(2)
    nblk = m // ATT_BQ
    lane = lax.broadcasted_iota(jnp.int32, (1, LANES), 1)
    lo = lane < ATT_HEAD_DIM
    scale = ATT_HEAD_DIM ** -0.5
    qmask = (jnp.where(lo, scale, 0.0).astype(BF16), jnp.where(lo, 0.0, scale).astype(BF16))

    def body(i, carry):
        blk = qi * (tq // ATT_BQ) + i
        q0 = pl.multiple_of(i * ATT_BQ, ATT_BQ)
        k0 = pl.multiple_of(jnp.clip(blk * ATT_BQ - ATT_HALF, 0, m - ATT_BK), ATT_HALF)
        var = jnp.where(blk == 0, 0, jnp.where(blk == nblk - 1, 2, 1))
        rq = pl.ds(q0, ATT_BQ)
        rk = pl.ds(k0, ATT_BK)
        cols = [slice(pair * LANES, (pair + 1) * LANES) for pair in range(ATT_HEADS_PER_GROUP // 2)]
        scores = []
        for pair, cs in enumerate(cols):
            q2 = q_ref[rq, cs]
            k2 = k_ref[rk, cs]
            for sub in range(2):
                scores.append(_dot_nt(q2 * qmask[sub], k2) + tab_ref[var, 2 * pair + sub])
        probs = []
        for s in scores:
            mx = jnp.max(s, axis=-1, keepdims=True)
            p = jnp.exp(s - mx)
            l = jnp.sum(p, axis=-1, keepdims=True)
            probs.append((p.astype(BF16), l, mx + jnp.log(l)))
        for pair, cs in enumerate(cols):
            v2 = v_ref[rk, cs]
            (p0, l0, lse0), (p1, l1, lse1) = probs[2 * pair], probs[2 * pair + 1]
            o_ref[rq, cs] = jnp.where(lo, _dot(p0, v2) / l0, _dot(p1, v2) / l1)
            lse_ref[rq, cs] = jnp.where(lo, lse0, lse1)
        return carry

    lax.fori_loop(0, tq // ATT_BQ, body, 0, unroll=2)


def _attn_group(qkv5, tab, g, dil, batch, seq):
    m = seq // dil
    tq = min(m, 1024)
    q_spec = pl.BlockSpec((None, None, None, tq, ATT_WIDTH), lambda b, r, i: (0, b, r, i, 0))
    k_spec = pl.BlockSpec((None, None, None, m, ATT_WIDTH), lambda b, r, i: (1, b, r, 0, 0))
    v_spec = pl.BlockSpec((None, None, None, m, ATT_WIDTH), lambda b, r, i: (2, b, r, 0, 0))
    o_spec = pl.BlockSpec((None, None, tq, ATT_WIDTH), lambda b, r, i: (b, r, i, 0))
    tab_spec = pl.BlockSpec(tab.shape, lambda b, r, i: (0, 0, 0, 0))
    out_shape = jax.ShapeDtypeStruct((batch, dil, m, ATT_WIDTH), F32)
    return pl.pallas_call(
        functools.partial(_attn_kernel, m=m, tq=tq),
        grid=(batch, dil, m // tq),
        in_specs=[q_spec, k_spec, v_spec, tab_spec],
        out_specs=[o_spec, o_spec],
        out_shape=[out_shape, out_shape],
        compiler_params=pltpu.CompilerParams(
            dimension_semantics=("arbitrary", "arbitrary", "arbitrary"),
            vmem_limit_bytes=VMEM_LIMIT),
        name=f"dilated_attn_g{g}",
    )(qkv5, qkv5, qkv5, tab)


def _dilated_attention(qkv_groups, batch, seq):
    slopes = 2.0 ** (-ALIBI_MAX_EXP * jnp.arange(1, N_ATT_HEADS + 1, dtype=F32) / N_ATT_HEADS)
    slopes = slopes.reshape(N_GROUPS, ATT_HEADS_PER_GROUP)
    outs = []
    for g, (window, dil) in enumerate(ATT_GROUPS):
        assert window // (2 * dil) == ATT_HALF
        outs.append(_attn_group(qkv_groups[g], _attn_bias_tables(slopes[g], dil), g, dil, batch, seq))
    return outs


def _split3(x):
    hi = x.astype(BF16)
    r1 = x - hi.astype(F32)
    mid = r1.astype(BF16)
    lo = (r1 - mid.astype(F32)).astype(BF16)
    return hi, mid, lo


def _hgrn_kernel(hq_ref, hf_ref, hb_ref, hi_ref, hg_ref, lbp_ref, ng_ref, out_ref,
                 o_scr, carry_scr, *, seq):
    c = HGRN_CHUNK
    grp = HGRN_GROUP
    gr = grp * c
    n_groups = seq // gr
    row = lax.broadcasted_iota(jnp.int32, (c, c), 0)
    col = lax.broadcasted_iota(jnp.int32, (c, c), 1)
    tri = (row >= col).astype(BF16)
    tri3 = jnp.concatenate([tri, tri, tri], axis=1)
    causal = (row >= col, row <= col)
    qscale = HGRN_KEY ** -0.5

    def lower_bound(d):
        a = lbp_ref[d]
        mx = jnp.max(a, axis=0, keepdims=True)
        e = jnp.exp(a - mx)
        return e[0:1] / jnp.sum(e, axis=0, keepdims=True)

    lbs = (lower_bound(0), lower_bound(1))
    carry_scr[...] = jnp.zeros_like(carry_scr)
    o_scr[...] = jnp.zeros_like(o_scr)

    def chunk_rows(x, g):
        return x[g * c:(g + 1) * c]

    def stage_elementwise(d, t):
        r = t["rows"]
        xq, x = hq_ref[r, :], (hf_ref, hb_ref)[d][r, :]
        lb = lbs[d]
        f = lb + (1.0 - lb) * jax.nn.sigmoid(x)
        t.update(q=xq * jax.nn.sigmoid(xq) * qscale, kk=1.0 - f, lf=jnp.log(f), v=hi_ref[r, :])

    def stage_prefix(d, t):
        hi, mid, lo = _split3(t["lf"])
        t["pre"] = _dot(tri3, jnp.concatenate([hi, mid, lo], axis=0))

    def stage_decay(d, t):
        pre = t["pre"]
        tot = pre[c - 1:c]
        if d == 0:
            cum, cl = pre, tot
        else:
            cum = tot - pre + t["lf"]
            cl = cum[0:1]
        ec, ecl = jnp.exp(cum), jnp.exp(cl)
        kinv = t["kk"] / ec
        t.update(qd=(t["q"] * ec).astype(BF16), ki=kinv.astype(BF16),
                 kd=(kinv * ecl).astype(BF16), dec=ecl)

    def stage_local_dots(d, t):
        t["a"] = _dot_nt(t["qd"], t["ki"])
        t["ut"] = _dot_tn(t["v"], t["kd"])

    def stage_state(d, t, st):
        t["entry"] = st.astype(BF16)
        return st * t["dec"] + t["ut"]

    def stage_output(d, t):
        o = _dot(jnp.where(causal[d], t["a"], 0.0).astype(BF16), t["v"]) + _dot_nt(t["qd"], t["entry"])
        o_scr[t["rows"], :] += o

    stages = (stage_elementwise, stage_prefix, stage_decay, stage_local_dots, None, stage_output)

    def scan_body(n, carry):
        base = (n * gr, (n_groups - 1 - n) * gr)
        order = (list(range(grp)), list(reversed(range(grp))))
        tasks = [[{"rows": pl.ds(pl.multiple_of(base[d] + g * c, c), c)} for g in order[d]] for d in (0, 1)]
        st = [carry_scr[0], carry_scr[1]]
        for slot in range(grp + len(stages) - 1):
            for k, stage in enumerate(stages):
                i = slot - k
                if 0 <= i < grp:
                    for d in (0, 1):
                        if stage is None:
                            st[d] = stage_state(d, tasks[d][i], st[d])
                        else:
                            stage(d, tasks[d][i])
        carry_scr[0] = st[0]
        carry_scr[1] = st[1]
        return carry

    lax.fori_loop(0, n_groups, scan_body, 0)

    def out_body(n, carry):
        r = pl.ds(pl.multiple_of(n * gr, gr), gr)
        o = o_scr[r, :]
        o = o * lax.rsqrt(jnp.mean(o * o, axis=-1, keepdims=True) + RMS_EPS) * ng_ref[...]
        xg = hg_ref[r, :]
        out_ref[r, :] = (o * (xg * jax.nn.sigmoid(xg))).astype(BF16)
        return carry

    lax.fori_loop(0, n_groups, out_body, 0)


def _hgrn(proj_a, proj_d, hgrn_lb, norm_g, batch, seq):
    a4 = proj_a.reshape(proj_a.shape[0], batch, seq, PROJ_TN)
    d4 = proj_d.reshape(proj_d.shape[0], batch, seq, PROJ_TN)
    nh = HGRN_HEADS
    hpt = PROJ_TN // LANES
    tpf = D_MODEL // PROJ_TN

    def col(field, tile0=0):
        return pl.BlockSpec((None, None, seq, LANES),
                            lambda b, h: (tile0 + field * tpf + h // hpt, b, 0, h % hpt))

    out = pl.pallas_call(
        functools.partial(_hgrn_kernel, seq=seq),
        grid=(batch, nh),
        in_specs=[
            col(0), col(1), col(2),
            col(0, A_HI_TILE),
            col(3),
            pl.BlockSpec((2, DEPTH + 1, LANES), lambda b, h: (0, 0, h)),
            pl.BlockSpec((1, HGRN_VAL), lambda b, h: (0, 0)),
        ],
        out_specs=pl.BlockSpec((None, seq, LANES), lambda b, h: (b, 0, h)),
        out_shape=jax.ShapeDtypeStruct((batch, seq, D_MODEL), BF16),
        scratch_shapes=[
            pltpu.VMEM((seq, HGRN_VAL), F32),
            pltpu.VMEM((2, HGRN_VAL, HGRN_KEY), F32),
        ],
        compiler_params=pltpu.CompilerParams(
            dimension_semantics=("arbitrary", "arbitrary"), vmem_limit_bytes=VMEM_LIMIT),
        name="hgrn2",
    )(d4, d4, d4, a4, d4, hgrn_lb, norm_g)
    return out.reshape(batch * seq, D_MODEL)


def _merge_kernel(x_ref, lg_ref, lb_ref, o0_ref, l0_ref, o1_ref, l1_ref, o2_ref, l2_ref, rec_ref,
                  wg_ref, wa_ref, wr_ref, wo_ref, g1_ref, b1_ref, out_ref,
                  o1_scr, l1_scr, o2_scr, l2_scr):
    tm = MERGE_TM
    for src, dst, dil in ((o1_ref, o1_scr, ATT_GROUPS[1][1]), (l1_ref, l1_scr, ATT_GROUPS[1][1]),
                          (o2_ref, o2_scr, ATT_GROUPS[2][1]), (l2_ref, l2_scr, ATT_GROUPS[2][1])):
        for r in range(dil):
            for c in range(ATT_WIDTH // LANES):
                dst[c, pl.ds(r, tm // dil, stride=dil), :] = src[r, :, c * LANES:(c + 1) * LANES]

    def token_order(scr, r):
        return jnp.concatenate([scr[c, r, :] for c in range(ATT_WIDTH // LANES)], axis=1)

    def attention(r):
        l0, l1, l2 = l0_ref[r, :], token_order(l1_scr, r), token_order(l2_scr, r)
        mx = jnp.maximum(jnp.maximum(l0, l1), l2)
        e0, e1, e2 = jnp.exp(l0 - mx), jnp.exp(l1 - mx), jnp.exp(l2 - mx)
        return (e0 * o0_ref[r, :] + e1 * token_order(o1_scr, r) + e2 * token_order(o2_scr, r)) / (e0 + e1 + e2)

    rs = tm // MERGE_SUB
    subs = [slice(s * rs, (s + 1) * rs) for s in range(MERGE_SUB)]
    ur = [_dot(rec_ref[r, :], wr_ref[...]) for r in subs]
    h = [_layer_norm(x_ref[r, :], lg_ref[...], lb_ref[...]) for r in subs]
    gates = [_dot(hs.astype(BF16), wg_ref[...]) for hs in h]
    ua = [_dot(attention(r).astype(BF16), wa_ref[...]) for r in subs]
    merged = [jax.nn.sigmoid(g[:, :D_MODEL]) * a + jax.nn.sigmoid(g[:, D_MODEL:]) * u
              for g, a, u in zip(gates, ua, ur)]
    z = [_dot(m.astype(BF16), wo_ref[...]) for m in merged]
    for r, hs, zs in zip(subs, h, z):
        out_ref[r, :] = _layer_norm(DEEPNORM_ALPHA * hs + zs, g1_ref[...], b1_ref[...])


def _const_spec(shape):
    return pl.BlockSpec(shape, lambda i: (0,) * len(shape))


def _merge(x2, lg, lb, att_groups, rec, wg, wa, wr, wo, g1, b1, batch, seq):
    t = x2.shape[0]
    tm = MERGE_TM
    tpb = seq // tm
    att_args, att_specs = [], []
    for (o, lse), (_, dil) in zip(att_groups, ATT_GROUPS):
        if dil == 1:
            spec = pl.BlockSpec((tm, ATT_WIDTH), lambda i: (i, 0))
            o, lse = o.reshape(t, ATT_WIDTH), lse.reshape(t, ATT_WIDTH)
        else:
            spec = pl.BlockSpec((None, dil, tm // dil, ATT_WIDTH), lambda i: (i // tpb, 0, i % tpb, 0))
        att_args += [o, lse]
        att_specs += [spec, spec]
    return pl.pallas_call(
        _merge_kernel,
        grid=(t // tm,),
        in_specs=[
            pl.BlockSpec((tm, D_MODEL), lambda i: (i, 0)),
            _const_spec((1, D_MODEL)), _const_spec((1, D_MODEL)),
            *att_specs,
            pl.BlockSpec((tm, D_MODEL), lambda i: (i, 0)),
            _const_spec(wg.shape), _const_spec(wa.shape), _const_spec(wr.shape), _const_spec(wo.shape),
            _const_spec((1, D_MODEL)), _const_spec((1, D_MODEL)),
        ],
        out_specs=pl.BlockSpec((tm, D_MODEL), lambda i: (i, 0)),
        out_shape=jax.ShapeDtypeStruct((t, D_MODEL), F32),
        scratch_shapes=[pltpu.VMEM((ATT_WIDTH // LANES, tm, LANES), F32)] * 4,
        compiler_params=pltpu.CompilerParams(
            dimension_semantics=("arbitrary",), vmem_limit_bytes=VMEM_LIMIT),
        name="merge_ln1",
    )(x2, lg, lb, *att_args, rec, wg, wa, wr, wo, g1, b1)


def _ffn_kernel(h_ref, wi_ref, wo_ref, g2_ref, b2_ref, out_ref, a_scr):
    h = h_ref[...]
    hb = h.astype(BF16)
    for c in range(D_FF // FFN_TC):
        gate = _dot(hb, wi_ref[:, c * FFN_TC:(c + 1) * FFN_TC])
        up = _dot(hb, wi_ref[:, D_FF + c * FFN_TC:D_FF + (c + 1) * FFN_TC])
        a_scr[:, c * FFN_TC:(c + 1) * FFN_TC] = (gate * jax.nn.sigmoid(gate) * up).astype(BF16)
    y = DEEPNORM_ALPHA * h + _dot(a_scr[...], wo_ref[...])
    out_ref[...] = _layer_norm(y, g2_ref[...], b2_ref[...])


def _ffn(h1, wi, wo, g2, b2):
    t = h1.shape[0]
    tm = FFN_TM
    return pl.pallas_call(
        _ffn_kernel,
        grid=(t // tm,),
        in_specs=[
            pl.BlockSpec((tm, D_MODEL), lambda i: (i, 0)),
            _const_spec(wi.shape), _const_spec(wo.shape),
            _const_spec((1, D_MODEL)), _const_spec((1, D_MODEL)),
        ],
        out_specs=pl.BlockSpec((tm, D_MODEL), lambda i: (i, 0)),
        out_shape=jax.ShapeDtypeStruct((t, D_MODEL), F32),
        scratch_shapes=[pltpu.VMEM((tm, D_FF), BF16)],
        compiler_params=pltpu.CompilerParams(
            dimension_semantics=("arbitrary",), vmem_limit_bytes=VMEM_LIMIT),
        name="ffn_ln2",
    )(h1, wi, wo, g2, b2)


def kernel(x, ln_in_g, ln_in_b, w_in, hgrn_lb, hgrn_norm_g, w_att_up, w_hgrn_up, w_o,
           ln1_g, ln1_b, w_ffn_in, w_ffn_out, ln2_g, ln2_b):
    batch, seq, d = x.shape
    assert d == D_MODEL and w_in.shape == (DEPTH, D_MODEL, IN_COLS)
    assert seq % PROJ_TM == 0 and seq % (16 * ATT_BK) == 0
    x2 = x.reshape(batch * seq, d)
    row = lambda v: v.reshape(1, -1).astype(F32)
    lg, lb = row(ln_in_g), row(ln_in_b)
    w_in_b = w_in[0].astype(BF16)

    proj_a, proj_b, proj_c, proj_d = _ln_proj(x2, lg, lb, w_in_b, batch, seq)
    att_groups = _dilated_attention(
        (proj_a.reshape(proj_a.shape[0], batch, 1, seq, PROJ_TN), proj_b, proj_c), batch, seq)
    rec = _hgrn(proj_a, proj_d, hgrn_lb.astype(F32), row(hgrn_norm_g[0]), batch, seq)
    h1 = _merge(x2, lg, lb, att_groups, rec, w_in_b[:, GATE_COL0:],
                w_att_up[0].astype(BF16), w_hgrn_up[0].astype(BF16), w_o[0].astype(BF16),
                row(ln1_g[0]), row(ln1_b[0]), batch, seq)
    out = _ffn(h1, w_ffn_in[0].astype(BF16), w_ffn_out[0].astype(BF16), row(ln2_g[0]), row(ln2_b[0]))
    return out.reshape(batch, seq, d)
```

```python
import functools

import jax
import jax.numpy as jnp
import numpy as np
from jax import lax
from jax.experimental import pallas as pl
from jax.experimental.pallas import tpu as pltpu

F32 = jnp.float32
BF16 = jnp.bfloat16

D_MODEL = 1024
DEPTH = 1
ATT_GROUPS = ((128, 1), (512, 4), (2048, 16))
N_GROUPS = len(ATT_GROUPS)
ATT_HEADS_PER_GROUP = 8
ATT_HEAD_DIM = 64
N_ATT_HEADS = N_GROUPS * ATT_HEADS_PER_GROUP
ATT_QKV = N_ATT_HEADS * ATT_HEAD_DIM
ATT_WIDTH = ATT_HEADS_PER_GROUP * ATT_HEAD_DIM
ALIBI_MAX_EXP = 8.0
NEG_INF = -1e30
HGRN_HEADS = 8
HGRN_KEY = 128
HGRN_VAL = 128
HGRN_CHUNK = 64
D_FF = 2816
DEEPNORM_ALPHA = (2.0 * DEPTH) ** 0.25
LN_EPS = 1e-5
RMS_EPS = 1e-6
QKV_COLS = 3 * ATT_QKV
REC_COLS = 3 * D_MODEL
GATE_COLS = 3 * D_MODEL
IN_COLS = QKV_COLS + REC_COLS + D_MODEL + GATE_COLS

LANES = 128
VMEM_LIMIT = 56 * 1024 * 1024
PROJ_TM = 2048
PROJ_TN = 512
PROJ_MM = 512
ATT_BQ = 128
ATT_BK = 256
ATT_HALF = 64
ATT_TS = 2048
HGRN_GROUP = 64
MERGE_TM = 512
MERGE_SUB = 2
FFN_TM = 512
FFN_TC = 256


def _layer_norm(x, g, b):
    mu = jnp.mean(x, axis=-1, keepdims=True)
    xc = x - mu
    var = jnp.mean(xc * xc, axis=-1, keepdims=True)
    return xc * lax.rsqrt(var + LN_EPS) * g + b


def _dot(a, b):
    return jnp.dot(a, b, preferred_element_type=F32)


def _dot_nt(a, b):
    return lax.dot_general(a, b, (((1,), (1,)), ((), ())), preferred_element_type=F32)


def _dot_tn(a, b):
    return lax.dot_general(a, b, (((0,), (0,)), ((), ())), preferred_element_type=F32)


_PROJ_STEPS = (("D", 9), ("A", 0), ("D", 10), ("A", 3), ("D", 11), ("A", 6), ("D", 12), ("A", 15),
               ("D", 13), ("A", 16), ("D", 14), ("B", 1), ("D", 17), ("B", 4), ("D", 18), ("B", 7),
               ("C", 2), ("C", 5), ("C", 8))
_NJ = len(_PROJ_STEPS)
_KINDS = "ABCD"
_N_TILES = {kind: sum(k == kind for k, _ in _PROJ_STEPS) for kind in "ABCD"}
A_HI_TILE = 3
GATE_COL0 = QKV_COLS + REC_COLS + 2 * D_MODEL
_TAB_WCOL, _TAB_KIND, _TAB_BLOCK = 0, 1, 2


def _proj_tables():
    rows = [[w for _, w in _PROJ_STEPS], [_KINDS.index(k) for k, _ in _PROJ_STEPS]]
    for kind in _KINDS:
        n, blocks = 0, []
        for k, _ in _PROJ_STEPS:
            n += k == kind
            blocks.append(max(n - 1, 0))
        rows.append(blocks)
    return np.asarray(rows, np.int32)


def _ln_proj_kernel(tab_ref, x_ref, g_ref, b_ref, w_ref, a_ref, b_out_ref, c_out_ref, d_ref,
                    h_scr, hperm_scr, acc_scr):
    j = pl.program_id(1)
    kind = tab_ref[_TAB_KIND, j]
    rows = 256
    d1, d2 = ATT_GROUPS[1][1], ATT_GROUPS[2][1]
    pblk = d2 * d2

    @pl.when(j == 0)
    def _():
        def body(c, carry):
            r = pl.ds(pl.multiple_of(c * rows, rows), rows)
            h_scr[r, :] = _layer_norm(x_ref[r, :], g_ref[...], b_ref[...]).astype(BF16)
            return carry
        lax.fori_loop(0, PROJ_TM // rows, body, 0)

        ri = lax.broadcasted_iota(jnp.int32, (pblk, pblk), 0)
        ci = lax.broadcasted_iota(jnp.int32, (pblk, pblk), 1)
        perm = (ci == (ri % d2) * d2 + ri // d2).astype(BF16)

        def permute(c, carry):
            r = pl.ds(pl.multiple_of(c * pblk, pblk), pblk)
            hperm_scr[r, :] = _dot(perm, h_scr[r, :]).astype(BF16)
            return carry
        lax.fori_loop(0, PROJ_TM // pblk, permute, 0)

    row_chunks = [slice(c * PROJ_MM, (c + 1) * PROJ_MM) for c in range(PROJ_TM // PROJ_MM)]

    @pl.when(kind == _KINDS.index("A"))
    def _():
        for rc in row_chunks:
            a_ref[rc, :] = _dot(h_scr[rc, :], w_ref[...]).astype(BF16)

    @pl.when(kind == _KINDS.index("B"))
    def _():
        n = PROJ_TM // d1
        for rc in row_chunks:
            acc = _dot(h_scr[rc, :], w_ref[...])
            for c in range(PROJ_TN // LANES):
                acc_scr[c, rc, :] = acc[:, c * LANES:(c + 1) * LANES]
        for c in range(PROJ_TN // LANES):
            for r in range(d1):
                b_out_ref[r, :, c * LANES:(c + 1) * LANES] = (
                    acc_scr[c, pl.ds(r, n, stride=d1), :].astype(BF16))

    @pl.when(kind == _KINDS.index("C"))
    def _():
        for ci, rc in enumerate(row_chunks):
            acc = _dot(hperm_scr[rc, :], w_ref[...]).astype(BF16)
            for b in range(PROJ_MM // pblk):
                blk = ci * (PROJ_MM // pblk) + b
                for r in range(d2):
                    src = b * pblk + r * d2
                    c_out_ref[r, blk * d2:(blk + 1) * d2, :] = acc[src:src + d2, :]

    @pl.when(kind == _KINDS.index("D"))
    def _():
        for rc in row_chunks:
            d_ref[rc, :] = _dot(h_scr[rc, :], w_ref[...])


def _ln_proj(x2, g, b, w_in, batch, seq):
    t = x2.shape[0]
    tpb = seq // PROJ_TM
    d1, d2 = ATT_GROUPS[1][1], ATT_GROUPS[2][1]

    def block(tab, kind, j):
        return tab[_TAB_BLOCK + _KINDS.index(kind), j]

    def token_spec(kind):
        return pl.BlockSpec((None, PROJ_TM, PROJ_TN), lambda i, j, tab: (block(tab, kind, j), i, 0))

    def stream_spec(dil, kind):
        return pl.BlockSpec((None, None, dil, PROJ_TM // dil, PROJ_TN),
                            lambda i, j, tab: (block(tab, kind, j), i // tpb, 0, i % tpb, 0))

    grid_spec = pltpu.PrefetchScalarGridSpec(
        num_scalar_prefetch=1,
        grid=(t // PROJ_TM, _NJ),
        in_specs=[
            pl.BlockSpec((PROJ_TM, D_MODEL), lambda i, j, tab: (i, 0)),
            pl.BlockSpec((1, D_MODEL), lambda i, j, tab: (0, 0)),
            pl.BlockSpec((1, D_MODEL), lambda i, j, tab: (0, 0)),
            pl.BlockSpec((D_MODEL, PROJ_TN), lambda i, j, tab: (0, tab[_TAB_WCOL, j])),
        ],
        out_specs=[
            token_spec("A"),
            stream_spec(d1, "B"),
            stream_spec(d2, "C"),
            token_spec("D"),
        ],
        scratch_shapes=[pltpu.VMEM((PROJ_TM, D_MODEL), BF16),
                        pltpu.VMEM((PROJ_TM, D_MODEL), BF16),
                        pltpu.VMEM((PROJ_TN // LANES, PROJ_TM, LANES), F32)],
    )
    return pl.pallas_call(
        _ln_proj_kernel,
        grid_spec=grid_spec,
        out_shape=[
            jax.ShapeDtypeStruct((_N_TILES["A"], t, PROJ_TN), BF16),
            jax.ShapeDtypeStruct((_N_TILES["B"], batch, d1, seq // d1, PROJ_TN), BF16),
            jax.ShapeDtypeStruct((_N_TILES["C"], batch, d2, seq // d2, PROJ_TN), BF16),
            jax.ShapeDtypeStruct((_N_TILES["D"], t, PROJ_TN), F32),
        ],
        compiler_params=pltpu.CompilerParams(
            dimension_semantics=("arbitrary", "arbitrary"), vmem_limit_bytes=VMEM_LIMIT),
        name="ln_proj",
    )(jnp.asarray(_proj_tables()), x2, g, b, w_in)


def _attn_bias_tables(slopes, dil):
    ql = jnp.arange(ATT_BQ)[:, None]
    kl = jnp.arange(ATT_BK)[None, :]
    tabs = []
    for off in (0, -ATT_HALF, -2 * ATT_HALF):
        rel = kl - ql + off
        valid = jnp.abs(rel) <= ATT_HALF
        dist = (dil * jnp.abs(rel)).astype(F32)
        bias = -slopes.astype(F32)[:, None, None] * dist[None]
        tabs.append(jnp.where(valid[None], bias, NEG_INF))
    return jnp.stack(tabs)


def _attn_kernel(q_ref, k_ref, v_ref, tab_ref, o_ref, lse_ref, *, m, dil, bps):
    tile = pl.program_id(1)
    nblk = m // ATT_BQ
    lane = lax.broadcasted_iota(jnp.int32, (1, LANES), 1)
    lo = lane < ATT_HEAD_DIM
    scale = ATT_HEAD_DIM ** -0.5
    qmask = (jnp.where(lo, scale, 0.0).astype(BF16), jnp.where(lo, 0.0, scale).astype(BF16))

    def body(n, carry):
        r = n // bps
        i = n % bps
        blk = tile * bps + i
        q0 = pl.multiple_of(i * ATT_BQ, ATT_BQ)
        k0 = pl.multiple_of(jnp.clip(blk * ATT_BQ - ATT_HALF, 0, m - ATT_BK), ATT_HALF)
        var = jnp.where(blk == 0, 0, jnp.where(blk == nblk - 1, 2, 1))
        rq = pl.ds(q0, ATT_BQ)
        rk = pl.ds(k0, ATT_BK)
        ro = pl.ds(i * (ATT_BQ * dil) + r, ATT_BQ, stride=dil) if dil > 1 else rq
        cols = [slice(pair * LANES, (pair + 1) * LANES) for pair in range(ATT_HEADS_PER_GROUP // 2)]
        scores = []
        for pair, cs in enumerate(cols):
            q2 = q_ref[r, rq, cs]
            k2 = k_ref[r, rk, cs]
            for sub in range(2):
                scores.append(_dot_nt(q2 * qmask[sub], k2) + tab_ref[var, 2 * pair + sub])
        probs = []
        for s in scores:
            mx = jnp.max(s, axis=-1, keepdims=True)
            p = jnp.exp(s - mx)
            l = jnp.sum(p, axis=-1, keepdims=True)
            probs.append((p.astype(BF16), l, mx + jnp.log(l)))
        for pair, cs in enumerate(cols):
            v2 = v_ref[r, rk, cs]
            (p0, l0, lse0), (p1, l1, lse1) = probs[2 * pair], probs[2 * pair + 1]
            o_ref[pair, ro, :] = jnp.where(lo, _dot(p0, v2) / l0, _dot(p1, v2) / l1)
            lse_ref[pair, ro, :] = jnp.where(lo, lse0, lse1)
        return carry

    lax.fori_loop(0, dil * bps, body, 0, unroll=2)


def _attn_group(qkv5, tab, g, dil, batch, seq):
    m = seq // dil
    ts = ATT_TS
    p = ts // dil
    tpb = seq // ts
    nlb = ATT_WIDTH // LANES
    q_spec = pl.BlockSpec((None, None, dil, p, ATT_WIDTH), lambda b, i: (0, b, 0, i, 0))
    k_spec = pl.BlockSpec((None, None, dil, m, ATT_WIDTH), lambda b, i: (1, b, 0, 0, 0))
    v_spec = pl.BlockSpec((None, None, dil, m, ATT_WIDTH), lambda b, i: (2, b, 0, 0, 0))
    o_spec = pl.BlockSpec((nlb, ts, LANES), lambda b, i: (0, b * tpb + i, 0))
    tab_spec = pl.BlockSpec(tab.shape, lambda b, i: (0, 0, 0, 0))
    out_shape = jax.ShapeDtypeStruct((nlb, batch * seq, LANES), F32)
    return pl.pallas_call(
        functools.partial(_attn_kernel, m=m, dil=dil, bps=p // ATT_BQ),
        grid=(batch, tpb),
        in_specs=[q_spec, k_spec, v_spec, tab_spec],
        out_specs=[o_spec, o_spec],
        out_shape=[out_shape, out_shape],
        compiler_params=pltpu.CompilerParams(
            dimension_semantics=("arbitrary", "arbitrary"), vmem_limit_bytes=VMEM_LIMIT),
        name=f"dilated_attn_g{g}",
    )(qkv5, qkv5, qkv5, tab)


def _dilated_attention(qkv_groups, batch, seq):
    slopes = 2.0 ** (-ALIBI_MAX_EXP * jnp.arange(1, N_ATT_HEADS + 1, dtype=F32) / N_ATT_HEADS)
    slopes = slopes.reshape(N_GROUPS, ATT_HEADS_PER_GROUP)
    outs = []
    for g, (window, dil) in enumerate(ATT_GROUPS):
        assert window // (2 * dil) == ATT_HALF
        outs.append(_attn_group(qkv_groups[g], _attn_bias_tables(slopes[g], dil), g, dil, batch, seq))
    return outs


def _split3(x):
    hi = x.astype(BF16)
    r1 = x - hi.astype(F32)
    mid = r1.astype(BF16)
    lo = (r1 - mid.astype(F32)).astype(BF16)
    return hi, mid, lo


def _hgrn_kernel(hq_ref, hf_ref, hb_ref, hi_ref, hg_ref, lbp_ref, ng_ref, out_ref,
                 o_scr, carry_scr, *, seq):
    c = HGRN_CHUNK
    grp = HGRN_GROUP
    gr = grp * c
    n_groups = seq // gr
    row = lax.broadcasted_iota(jnp.int32, (c, c), 0)
    col = lax.broadcasted_iota(jnp.int32, (c, c), 1)
    tri = (row >= col).astype(BF16)
    tri3 = jnp.concatenate([tri, tri, tri], axis=1)
    causal = (row >= col, row <= col)
    qscale = HGRN_KEY ** -0.5

    def lower_bound(d):
        a = lbp_ref[d]
        mx = jnp.max(a, axis=0, keepdims=True)
        e = jnp.exp(a - mx)
        return e[0:1] / jnp.sum(e, axis=0, keepdims=True)

    lbs = (lower_bound(0), lower_bound(1))
    carry_scr[...] = jnp.zeros_like(carry_scr)
    o_scr[...] = jnp.zeros_like(o_scr)

    def chunk_rows(x, g):
        return x[g * c:(g + 1) * c]

    def stage_elementwise(d, t):
        r = t["rows"]
        xq, x = hq_ref[r, :], (hf_ref, hb_ref)[d][r, :]
        lb = lbs[d]
        f = lb + (1.0 - lb) * jax.nn.sigmoid(x)
        t.update(q=xq * jax.nn.sigmoid(xq) * qscale, kk=1.0 - f, lf=jnp.log(f), v=hi_ref[r, :])

    def stage_prefix(d, t):
        hi, mid, lo = _split3(t["lf"])
        t["pre"] = _dot(tri3, jnp.concatenate([hi, mid, lo], axis=0))

    def stage_decay(d, t):
        pre = t["pre"]
        tot = pre[c - 1:c]
        if d == 0:
            cum, cl = pre, tot
        else:
            cum = tot - pre + t["lf"]
            cl = cum[0:1]
        ec, ecl = jnp.exp(cum), jnp.exp(cl)
        kinv = t["kk"] / ec
        t.update(qd=(t["q"] * ec).astype(BF16), ki=kinv.astype(BF16),
                 kd=(kinv * ecl).astype(BF16), dec=ecl)

    def stage_local_dots(d, t):
        t["a"] = _dot_nt(t["qd"], t["ki"])
        t["ut"] = _dot_tn(t["v"], t["kd"])

    def stage_state(d, t, st):
        t["entry"] = st.astype(BF16)
        return st * t["dec"] + t["ut"]

    def stage_output(d, t):
        o = _dot(jnp.where(causal[d], t["a"], 0.0).astype(BF16), t["v"]) + _dot_nt(t["qd"], t["entry"])
        o_scr[t["rows"], :] += o

    stages = (stage_elementwise, stage_prefix, stage_decay, stage_local_dots, None, stage_output)

    def scan_body(n, carry):
        base = (n * gr, (n_groups - 1 - n) * gr)
        order = (list(range(grp)), list(reversed(range(grp))))
        tasks = [[{"rows": pl.ds(pl.multiple_of(base[d] + g * c, c), c)} for g in order[d]] for d in (0, 1)]
        st = [carry_scr[0], carry_scr[1]]
        for slot in range(grp + len(stages) - 1):
            for k, stage in enumerate(stages):
                i = slot - k
                if 0 <= i < grp:
                    for d in (0, 1):
                        if stage is None:
                            st[d] = stage_state(d, tasks[d][i], st[d])
                        else:
                            stage(d, tasks[d][i])
        carry_scr[0] = st[0]
        carry_scr[1] = st[1]
        return carry

    lax.fori_loop(0, n_groups, scan_body, 0)

    def out_body(n, carry):
        r = pl.ds(pl.multiple_of(n * gr, gr), gr)
        o = o_scr[r, :]
        o = o * lax.rsqrt(jnp.mean(o * o, axis=-1, keepdims=True) + RMS_EPS) * ng_ref[...]
        xg = hg_ref[r, :]
        out_ref[r, :] = (o * (xg * jax.nn.sigmoid(xg))).astype(BF16)
        return carry

    lax.fori_loop(0, n_groups, out_body, 0)


def _hgrn(proj_a, proj_d, hgrn_lb, norm_g, batch, seq):
    a4 = proj_a.reshape(proj_a.shape[0], batch, seq, PROJ_TN)
    d4 = proj_d.reshape(proj_d.shape[0], batch, seq, PROJ_TN)
    nh = HGRN_HEADS
    hpt = PROJ_TN // LANES
    tpf = D_MODEL // PROJ_TN

    def col(field, tile0=0):
        return pl.BlockSpec((None, None, seq, LANES),
                            lambda b, h: (tile0 + field * tpf + h // hpt, b, 0, h % hpt))

    out = pl.pallas_call(
        functools.partial(_hgrn_kernel, seq=seq),
        grid=(batch, nh),
        in_specs=[
            col(0), col(1), col(2),
            col(0, A_HI_TILE),
            col(3),
            pl.BlockSpec((2, DEPTH + 1, LANES), lambda b, h: (0, 0, h)),
            pl.BlockSpec((1, HGRN_VAL), lambda b, h: (0, 0)),
        ],
        out_specs=pl.BlockSpec((None, seq, LANES), lambda b, h: (b, 0, h)),
        out_shape=jax.ShapeDtypeStruct((batch, seq, D_MODEL), BF16),
        scratch_shapes=[
            pltpu.VMEM((seq, HGRN_VAL), F32),
            pltpu.VMEM((2, HGRN_VAL, HGRN_KEY), F32),
        ],
        compiler_params=pltpu.CompilerParams(
            dimension_semantics=("arbitrary", "arbitrary"), vmem_limit_bytes=VMEM_LIMIT),
        name="hgrn2",
    )(d4, d4, d4, a4, d4, hgrn_lb, norm_g)
    return out.reshape(batch * seq, D_MODEL)


def _merge_kernel(x_ref, lg_ref, lb_ref, o0_ref, l0_ref, o1_ref, l1_ref, o2_ref, l2_ref, rec_ref,
                  wg_ref, wa_ref, wr_ref, wo_ref, g1_ref, b1_ref, out_ref):
    tm = MERGE_TM

    def rows_of(ref, r):
        return jnp.concatenate([ref[c, r, :] for c in range(ATT_WIDTH // LANES)], axis=1)

    def attention(r):
        l0, l1, l2 = rows_of(l0_ref, r), rows_of(l1_ref, r), rows_of(l2_ref, r)
        mx = jnp.maximum(jnp.maximum(l0, l1), l2)
        e0, e1, e2 = jnp.exp(l0 - mx), jnp.exp(l1 - mx), jnp.exp(l2 - mx)
        return (e0 * rows_of(o0_ref, r) + e1 * rows_of(o1_ref, r) + e2 * rows_of(o2_ref, r)) / (e0 + e1 + e2)

    rs = tm // MERGE_SUB
    subs = [slice(s * rs, (s + 1) * rs) for s in range(MERGE_SUB)]
    ur = [_dot(rec_ref[r, :], wr_ref[...]) for r in subs]
    h = [_layer_norm(x_ref[r, :], lg_ref[...], lb_ref[...]) for r in subs]
    gates = [_dot(hs.astype(BF16), wg_ref[...]) for hs in h]
    ua = [_dot(attention(r).astype(BF16), wa_ref[...]) for r in subs]
    merged = [jax.nn.sigmoid(g[:, :D_MODEL]) * a + jax.nn.sigmoid(g[:, D_MODEL:]) * u
              for g, a, u in zip(gates, ua, ur)]
    z = [_dot(m.astype(BF16), wo_ref[...]) for m in merged]
    for r, hs, zs in zip(subs, h, z):
        out_ref[r, :] = _layer_norm(DEEPNORM_ALPHA * hs + zs, g1_ref[...], b1_ref[...])


def _const_spec(shape):
    return pl.BlockSpec(shape, lambda i: (0,) * len(shape))


def _merge(x2, lg, lb, att_groups, rec, wg, wa, wr, wo, g1, b1, batch, seq):
    t = x2.shape[0]
    tm = MERGE_TM
    att_spec = pl.BlockSpec((ATT_WIDTH // LANES, tm, LANES), lambda i: (0, i, 0))
    att_args = [a for pair in att_groups for a in pair]
    att_specs = [att_spec] * len(att_args)
    return pl.pallas_call(
        _merge_kernel,
        grid=(t // tm,),
        in_specs=[
            pl.BlockSpec((tm, D_MODEL), lambda i: (i, 0)),
            _const_spec((1, D_MODEL)), _const_spec((1, D_MODEL)),
            *att_specs,
            pl.BlockSpec((tm, D_MODEL), lambda i: (i, 0)),
            _const_spec(wg.shape), _const_spec(wa.shape), _const_spec(wr.shape), _const_spec(wo.shape),
            _const_spec((1, D_MODEL)), _const_spec((1, D_MODEL)),
        ],
        out_specs=pl.BlockSpec((tm, D_MODEL), lambda i: (i, 0)),
        out_shape=jax.ShapeDtypeStruct((t, D_MODEL), F32),
        compiler_params=pltpu.CompilerParams(
            dimension_semantics=("arbitrary",), vmem_limit_bytes=VMEM_LIMIT),
        name="merge_ln1",
    )(x2, lg, lb, *att_args, rec, wg, wa, wr, wo, g1, b1)


def _ffn_kernel(h_ref, wi_ref, wo_ref, g2_ref, b2_ref, out_ref, a_scr):
    h = h_ref[...]
    hb = h.astype(BF16)
    for c in range(D_FF // FFN_TC):
        gate = _dot(hb, wi_ref[:, c * FFN_TC:(c + 1) * FFN_TC])
        up = _dot(hb, wi_ref[:, D_FF + c * FFN_TC:D_FF + (c + 1) * FFN_TC])
        a_scr[:, c * FFN_TC:(c + 1) * FFN_TC] = (gate * jax.nn.sigmoid(gate) * up).astype(BF16)
    y = DEEPNORM_ALPHA * h + _dot(a_scr[...], wo_ref[...])
    out_ref[...] = _layer_norm(y, g2_ref[...], b2_ref[...])


def _ffn(h1, wi, wo, g2, b2):
    t = h1.shape[0]
    tm = FFN_TM
    return pl.pallas_call(
        _ffn_kernel,
        grid=(t // tm,),
        in_specs=[
            pl.BlockSpec((tm, D_MODEL), lambda i: (i, 0)),
            _const_spec(wi.shape), _const_spec(wo.shape),
            _const_spec((1, D_MODEL)), _const_spec((1, D_MODEL)),
        ],
        out_specs=pl.BlockSpec((tm, D_MODEL), lambda i: (i, 0)),
        out_shape=jax.ShapeDtypeStruct((t, D_MODEL), F32),
        scratch_shapes=[pltpu.VMEM((tm, D_FF), BF16)],
        compiler_params=pltpu.CompilerParams(
            dimension_semantics=("arbitrary",), vmem_limit_bytes=VMEM_LIMIT),
        name="ffn_ln2",
    )(h1, wi, wo, g2, b2)


def kernel(x, ln_in_g, ln_in_b, w_in, hgrn_lb, hgrn_norm_g, w_att_up, w_hgrn_up, w_o,
           ln1_g, ln1_b, w_ffn_in, w_ffn_out, ln2_g, ln2_b):
    batch, seq, d = x.shape
    assert d == D_MODEL and w_in.shape == (DEPTH, D_MODEL, IN_COLS)
    assert seq % PROJ_TM == 0 and seq % (16 * ATT_BK) == 0
    x2 = x.reshape(batch * seq, d)
    row = lambda v: v.reshape(1, -1).astype(F32)
    lg, lb = row(ln_in_g), row(ln_in_b)
    w_in_b = w_in[0].astype(BF16)

    proj_a, proj_b, proj_c, proj_d = _ln_proj(x2, lg, lb, w_in_b, batch, seq)
    att_groups = _dilated_attention(
        (proj_a.reshape(proj_a.shape[0], batch, 1, seq, PROJ_TN), proj_b, proj_c), batch, seq)
    rec = _hgrn(proj_a, proj_d, hgrn_lb.astype(F32), row(hgrn_norm_g[0]), batch, seq)
    h1 = _merge(x2, lg, lb, att_groups, rec, w_in_b[:, GATE_COL0:],
                w_att_up[0].astype(BF16), w_hgrn_up[0].astype(BF16), w_o[0].astype(BF16),
                row(ln1_g[0]), row(ln1_b[0]), batch, seq)
    out = _ffn(h1, w_ffn_in[0].astype(BF16), w_ffn_out[0].astype(BF16), row(ln2_g[0]), row(ln2_b[0]))
    return out.reshape(batch, seq, d)
```

```python
import functools

import jax
import jax.numpy as jnp
import numpy as np
from jax import lax
from jax.experimental import pallas as pl
from jax.experimental.pallas import tpu as pltpu

F32 = jnp.float32
BF16 = jnp.bfloat16

D_MODEL = 1024
DEPTH = 1
ATT_GROUPS = ((128, 1), (512, 4), (2048, 16))
N_GROUPS = len(ATT_GROUPS)
ATT_HEADS_PER_GROUP = 8
ATT_HEAD_DIM = 64
N_ATT_HEADS = N_GROUPS * ATT_HEADS_PER_GROUP
ATT_QKV = N_ATT_HEADS * ATT_HEAD_DIM
ATT_WIDTH = ATT_HEADS_PER_GROUP * ATT_HEAD_DIM
ALIBI_MAX_EXP = 8.0
NEG_INF = -1e30
HGRN_HEADS = 8
HGRN_KEY = 128
HGRN_VAL = 128
HGRN_CHUNK = 64
D_FF = 2816
DEEPNORM_ALPHA = (2.0 * DEPTH) ** 0.25
LN_EPS = 1e-5
RMS_EPS = 1e-6
QKV_COLS = 3 * ATT_QKV
REC_COLS = 3 * D_MODEL
GATE_COLS = 3 * D_MODEL
IN_COLS = QKV_COLS + REC_COLS + D_MODEL + GATE_COLS

LANES = 128
VMEM_LIMIT = 56 * 1024 * 1024
PROJ_TM = 2048
PROJ_TN = 512
PROJ_MM = 512
ATT_BQ = 128
ATT_BK = 256
ATT_HALF = 64
ATT_TS = 2048
MERGE_TM = 512
MERGE_SUB = 2
FFN_TM = 512
FFN_TC = 256


def _layer_norm(x, g, b):
    mu = jnp.mean(x, axis=-1, keepdims=True)
    xc = x - mu
    var = jnp.mean(xc * xc, axis=-1, keepdims=True)
    return xc * lax.rsqrt(var + LN_EPS) * g + b


def _dot(a, b):
    return jnp.dot(a, b, preferred_element_type=F32)


def _dot_nt(a, b):
    return lax.dot_general(a, b, (((1,), (1,)), ((), ())), preferred_element_type=F32)


def _dot_tn(a, b):
    return lax.dot_general(a, b, (((0,), (0,)), ((), ())), preferred_element_type=F32)


_PROJ_STEPS = (("D", 9), ("A", 0), ("D", 10), ("A", 3), ("D", 11), ("A", 6), ("D", 12), ("A", 15),
               ("D", 13), ("A", 16), ("D", 14), ("B", 1), ("D", 17), ("B", 4), ("D", 18), ("B", 7),
               ("C", 2), ("C", 5), ("C", 8))
_NJ = len(_PROJ_STEPS)
assert _PROJ_STEPS[0][0] == "D"
_KINDS = "ABCD"
_N_TILES = {kind: sum(k == kind for k, _ in _PROJ_STEPS) for kind in "ABCD"}
A_HI_TILE = 3
GATE_COL0 = QKV_COLS + REC_COLS + 2 * D_MODEL
_TAB_WCOL, _TAB_KIND, _TAB_BLOCK = 0, 1, 2


def _proj_tables():
    rows = [[w for _, w in _PROJ_STEPS], [_KINDS.index(k) for k, _ in _PROJ_STEPS]]
    for kind in _KINDS:
        n, blocks = 0, []
        for k, _ in _PROJ_STEPS:
            n += k == kind
            blocks.append(max(n - 1, 0))
        rows.append(blocks)
    return np.asarray(rows, np.int32)


def _ln_proj_kernel(tab_ref, x_ref, g_ref, b_ref, w_ref, a_ref, b_out_ref, c_out_ref, d_ref,
                    h_scr, hperm_scr, acc_scr):
    j = pl.program_id(1)
    kind = tab_ref[_TAB_KIND, j]
    d1, d2 = ATT_GROUPS[1][1], ATT_GROUPS[2][1]
    pblk = d2 * d2

    @pl.when(j == 0)
    def _():
        ri = lax.broadcasted_iota(jnp.int32, (pblk, pblk), 0)
        ci = lax.broadcasted_iota(jnp.int32, (pblk, pblk), 1)
        perm = (ci == (ri % d2) * d2 + ri // d2).astype(BF16)
        per_mm = PROJ_MM // pblk
        hb = []
        for c in range(PROJ_TM // pblk + 1):
            if c < PROJ_TM // pblk:
                r = slice(c * pblk, (c + 1) * pblk)
                hb.append(_layer_norm(x_ref[r, :], g_ref[...], b_ref[...]).astype(BF16))
                h_scr[r, :] = hb[c]
            if c >= 1:
                hperm_scr[(c - 1) * pblk:c * pblk, :] = _dot(perm, hb[c - 1]).astype(BF16)
            if c >= 1 and c % per_mm == 0:
                rc = slice((c - per_mm) * pblk, c * pblk)
                d_ref[rc, :] = _dot(jnp.concatenate(hb[c - per_mm:c], axis=0), w_ref[...])

    row_chunks = [slice(c * PROJ_MM, (c + 1) * PROJ_MM) for c in range(PROJ_TM // PROJ_MM)]

    @pl.when(kind == _KINDS.index("A"))
    def _():
        for rc in row_chunks:
            a_ref[rc, :] = _dot(h_scr[rc, :], w_ref[...]).astype(BF16)

    @pl.when(kind == _KINDS.index("B"))
    def _():
        n = PROJ_TM // d1
        for rc in row_chunks:
            acc = _dot(h_scr[rc, :], w_ref[...])
            for c in range(PROJ_TN // LANES):
                acc_scr[c, rc, :] = acc[:, c * LANES:(c + 1) * LANES]
        for c in range(PROJ_TN // LANES):
            for r in range(d1):
                b_out_ref[r, :, c * LANES:(c + 1) * LANES] = (
                    acc_scr[c, pl.ds(r, n, stride=d1), :].astype(BF16))

    @pl.when(kind == _KINDS.index("C"))
    def _():
        for ci, rc in enumerate(row_chunks):
            acc = _dot(hperm_scr[rc, :], w_ref[...]).astype(BF16)
            for b in range(PROJ_MM // pblk):
                blk = ci * (PROJ_MM // pblk) + b
                for r in range(d2):
                    src = b * pblk + r * d2
                    c_out_ref[r, blk * d2:(blk + 1) * d2, :] = acc[src:src + d2, :]

    @pl.when((kind == _KINDS.index("D")) & (j > 0))
    def _():
        for rc in row_chunks:
            d_ref[rc, :] = _dot(h_scr[rc, :], w_ref[...])


def _ln_proj(x2, g, b, w_in, batch, seq):
    t = x2.shape[0]
    tpb = seq // PROJ_TM
    d1, d2 = ATT_GROUPS[1][1], ATT_GROUPS[2][1]

    def block(tab, kind, j):
        return tab[_TAB_BLOCK + _KINDS.index(kind), j]

    def token_spec(kind):
        return pl.BlockSpec((None, PROJ_TM, PROJ_TN), lambda i, j, tab: (block(tab, kind, j), i, 0))

    def stream_spec(dil, kind):
        return pl.BlockSpec((None, None, dil, PROJ_TM // dil, PROJ_TN),
                            lambda i, j, tab: (block(tab, kind, j), i // tpb, 0, i % tpb, 0))

    grid_spec = pltpu.PrefetchScalarGridSpec(
        num_scalar_prefetch=1,
        grid=(t // PROJ_TM, _NJ),
        in_specs=[
            pl.BlockSpec((PROJ_TM, D_MODEL), lambda i, j, tab: (i, 0)),
            pl.BlockSpec((1, D_MODEL), lambda i, j, tab: (0, 0)),
            pl.BlockSpec((1, D_MODEL), lambda i, j, tab: (0, 0)),
            pl.BlockSpec((D_MODEL, PROJ_TN), lambda i, j, tab: (0, tab[_TAB_WCOL, j])),
        ],
        out_specs=[
            token_spec("A"),
            stream_spec(d1, "B"),
            stream_spec(d2, "C"),
            token_spec("D"),
        ],
        scratch_shapes=[pltpu.VMEM((PROJ_TM, D_MODEL), BF16),
                        pltpu.VMEM((PROJ_TM, D_MODEL), BF16),
                        pltpu.VMEM((PROJ_TN // LANES, PROJ_TM, LANES), F32)],
    )
    return pl.pallas_call(
        _ln_proj_kernel,
        grid_spec=grid_spec,
        out_shape=[
            jax.ShapeDtypeStruct((_N_TILES["A"], t, PROJ_TN), BF16),
            jax.ShapeDtypeStruct((_N_TILES["B"], batch, d1, seq // d1, PROJ_TN), BF16),
            jax.ShapeDtypeStruct((_N_TILES["C"], batch, d2, seq // d2, PROJ_TN), BF16),
            jax.ShapeDtypeStruct((_N_TILES["D"], t, PROJ_TN), F32),
        ],
        compiler_params=pltpu.CompilerParams(
            dimension_semantics=("arbitrary", "arbitrary"), vmem_limit_bytes=VMEM_LIMIT),
        name="ln_proj",
    )(jnp.asarray(_proj_tables()), x2, g, b, w_in)


def _attn_bias_tables(slopes, dil):
    ql = jnp.arange(ATT_BQ)[:, None]
    kl = jnp.arange(ATT_BK)[None, :]
    tabs = []
    for off in (0, -ATT_HALF, -2 * ATT_HALF):
        rel = kl - ql + off
        valid = jnp.abs(rel) <= ATT_HALF
        dist = (dil * jnp.abs(rel)).astype(F32)
        bias = -slopes.astype(F32)[:, None, None] * dist[None]
        tabs.append(jnp.where(valid[None], bias, NEG_INF))
    return jnp.stack(tabs)


def _attn_kernel(q_ref, k_ref, v_ref, tab_ref, o_ref, lse_ref, *, m, dil, bps):
    tile = pl.program_id(1)
    nblk = m // ATT_BQ
    lane = lax.broadcasted_iota(jnp.int32, (1, LANES), 1)
    lo = lane < ATT_HEAD_DIM
    scale = ATT_HEAD_DIM ** -0.5
    qmask = (jnp.where(lo, scale, 0.0).astype(BF16), jnp.where(lo, 0.0, scale).astype(BF16))

    def body(n, carry):
        r = n // bps
        i = n % bps
        blk = tile * bps + i
        q0 = pl.multiple_of(i * ATT_BQ, ATT_BQ)
        k0 = pl.multiple_of(jnp.clip(blk * ATT_BQ - ATT_HALF, 0, m - ATT_BK), ATT_HALF)
        var = jnp.where(blk == 0, 0, jnp.where(blk == nblk - 1, 2, 1))
        rq = pl.ds(q0, ATT_BQ)
        rk = pl.ds(k0, ATT_BK)
        ro = pl.ds(i * (ATT_BQ * dil) + r, ATT_BQ, stride=dil) if dil > 1 else rq
        cols = [slice(pair * LANES, (pair + 1) * LANES) for pair in range(ATT_HEADS_PER_GROUP // 2)]
        scores = []
        for pair, cs in enumerate(cols):
            q2 = q_ref[r, rq, cs]
            k2 = k_ref[r, rk, cs]
            for sub in range(2):
                scores.append(_dot_nt(q2 * qmask[sub], k2) + tab_ref[var, 2 * pair + sub])
        probs = []
        for s in scores:
            mx = jnp.max(s, axis=-1, keepdims=True)
            p = jnp.exp(s - mx)
            l = jnp.sum(p, axis=-1, keepdims=True)
            probs.append((p.astype(BF16), l, mx + jnp.log(l)))
        for pair, cs in enumerate(cols):
            v2 = v_ref[r, rk, cs]
            (p0, l0, lse0), (p1, l1, lse1) = probs[2 * pair], probs[2 * pair + 1]
            o_ref[pair, ro, :] = jnp.where(lo, _dot(p0, v2) / l0, _dot(p1, v2) / l1)
            lse_ref[pair, ro, :] = jnp.where(lo, lse0, lse1)
        return carry

    lax.fori_loop(0, dil * bps, body, 0, unroll=2)


def _attn_group(qkv5, tab, g, dil, batch, seq):
    m = seq // dil
    ts = ATT_TS
    p = ts // dil
    tpb = seq // ts
    nlb = ATT_WIDTH // LANES
    q_spec = pl.BlockSpec((None, None, dil, p, ATT_WIDTH), lambda b, i: (0, b, 0, i, 0))
    k_spec = pl.BlockSpec((None, None, dil, m, ATT_WIDTH), lambda b, i: (1, b, 0, 0, 0))
    v_spec = pl.BlockSpec((None, None, dil, m, ATT_WIDTH), lambda b, i: (2, b, 0, 0, 0))
    o_spec = pl.BlockSpec((nlb, ts, LANES), lambda b, i: (0, b * tpb + i, 0))
    tab_spec = pl.BlockSpec(tab.shape, lambda b, i: (0, 0, 0, 0))
    out_shape = jax.ShapeDtypeStruct((nlb, batch * seq, LANES), F32)
    return pl.pallas_call(
        functools.partial(_attn_kernel, m=m, dil=dil, bps=p // ATT_BQ),
        grid=(batch, tpb),
        in_specs=[q_spec, k_spec, v_spec, tab_spec],
        out_specs=[o_spec, o_spec],
        out_shape=[out_shape, out_shape],
        compiler_params=pltpu.CompilerParams(
            dimension_semantics=("arbitrary", "arbitrary"), vmem_limit_bytes=VMEM_LIMIT),
        name=f"dilated_attn_g{g}",
    )(qkv5, qkv5, qkv5, tab)


def _dilated_attention(qkv_groups, batch, seq):
    slopes = 2.0 ** (-ALIBI_MAX_EXP * jnp.arange(1, N_ATT_HEADS + 1, dtype=F32) / N_ATT_HEADS)
    slopes = slopes.reshape(N_GROUPS, ATT_HEADS_PER_GROUP)
    outs = []
    for g, (window, dil) in enumerate(ATT_GROUPS):
        assert window // (2 * dil) == ATT_HALF
        outs.append(_attn_group(qkv_groups[g], _attn_bias_tables(slopes[g], dil), g, dil, batch, seq))
    return outs


def _split3(x):
    hi = x.astype(BF16)
    r1 = x - hi.astype(F32)
    mid = r1.astype(BF16)
    lo = (r1 - mid.astype(F32)).astype(BF16)
    return hi, mid, lo


def _hgrn_kernel(hq_ref, hf_ref, hb_ref, hi_ref, hg_ref, lbp_ref, ng_ref, out_ref, o_scr, *, seq):
    c = HGRN_CHUNK
    n_chunks = seq // c
    row = lax.broadcasted_iota(jnp.int32, (c, c), 0)
    col = lax.broadcasted_iota(jnp.int32, (c, c), 1)
    tri = (row >= col).astype(BF16)
    tri3 = jnp.concatenate([tri, tri, tri], axis=1)
    causal = (row >= col, row <= col)
    qscale = HGRN_KEY ** -0.5

    def lower_bound(d):
        a = lbp_ref[d]
        mx = jnp.max(a, axis=0, keepdims=True)
        e = jnp.exp(a - mx)
        return e[0:1] / jnp.sum(e, axis=0, keepdims=True)

    lbs = (lower_bound(0), lower_bound(1))
    q_cache = {}

    def stage_elementwise(d, t):
        r, g = t["rows"], t["chunk"]
        if g in q_cache:
            q = q_cache.pop(g)
        else:
            xq = hq_ref[r, :]
            q = q_cache[g] = xq * jax.nn.sigmoid(xq) * qscale
        lb = lbs[d]
        f = lb + (1.0 - lb) * jax.nn.sigmoid((hf_ref, hb_ref)[d][r, :])
        t.update(q=q, kk=1.0 - f, lf=jnp.log2(f), v=hi_ref[r, :])

    def stage_prefix(d, t):
        hi, mid, lo = _split3(t["lf"])
        t["pre"] = _dot(tri3, jnp.concatenate([hi, mid, lo], axis=0))

    def stage_decay(d, t):
        pre = t["pre"]
        tot = pre[c - 1:c]
        if d == 0:
            cum, cl = pre, tot
        else:
            cum = tot - pre + t["lf"]
            cl = cum[0:1]
        ec, ecl = jnp.exp2(cum), jnp.exp2(cl)
        kinv = t["kk"] / ec
        t.update(qd=(t["q"] * ec).astype(BF16), ki=kinv.astype(BF16),
                 kd=(kinv * ecl).astype(BF16), dec=ecl)

    def stage_local_dots(d, t):
        t["a"] = _dot_nt(t["qd"], t["ki"])
        t["ut"] = _dot_tn(t["v"], t["kd"])

    def stage_state(d, t, st):
        t["entry"] = st.astype(BF16)
        return st * t["dec"] + t["ut"]

    def stage_output(d, t):
        r, g = t["rows"], t["chunk"]
        o = _dot(jnp.where(causal[d], t["a"], 0.0).astype(BF16), t["v"]) + _dot_nt(t["qd"], t["entry"])
        fwd_first = g < n_chunks - 1 - g
        if (d == 0) == fwd_first:
            o_scr[r, :] = o
        else:
            o = o_scr[r, :] + o
            o = o * lax.rsqrt(jnp.mean(o * o, axis=-1, keepdims=True) + RMS_EPS) * ng_ref[...]
            xg = hg_ref[r, :]
            out_ref[r, :] = (o * (xg * jax.nn.sigmoid(xg))).astype(BF16)

    stages = (stage_elementwise, stage_prefix, stage_decay, stage_local_dots, None, stage_output)

    order = (list(range(n_chunks)), list(reversed(range(n_chunks))))
    tasks = [[{"chunk": g, "rows": slice(g * c, (g + 1) * c)} for g in order[d]] for d in (0, 1)]
    st = [jnp.zeros((HGRN_VAL, HGRN_KEY), F32), jnp.zeros((HGRN_VAL, HGRN_KEY), F32)]
    for slot in range(n_chunks + len(stages) - 1):
        for k, stage in enumerate(stages):
            i = slot - k
            if 0 <= i < n_chunks:
                for d in (0, 1):
                    if stage is None:
                        st[d] = stage_state(d, tasks[d][i], st[d])
                    else:
                        stage(d, tasks[d][i])
                        if stage is stage_output:
                            tasks[d][i].clear()


def _hgrn(proj_a, proj_d, hgrn_lb, norm_g, batch, seq):
    a4 = proj_a.reshape(proj_a.shape[0], batch, seq, PROJ_TN)
    d4 = proj_d.reshape(proj_d.shape[0], batch, seq, PROJ_TN)
    nh = HGRN_HEADS
    hpt = PROJ_TN // LANES
    tpf = D_MODEL // PROJ_TN

    def col(field, tile0=0):
        return pl.BlockSpec((None, None, seq, LANES),
                            lambda b, h: (tile0 + field * tpf + h // hpt, b, 0, h % hpt))

    out = pl.pallas_call(
        functools.partial(_hgrn_kernel, seq=seq),
        grid=(batch, nh),
        in_specs=[
            col(0), col(1), col(2),
            col(0, A_HI_TILE),
            col(3),
            pl.BlockSpec((2, DEPTH + 1, LANES), lambda b, h: (0, 0, h)),
            pl.BlockSpec((1, HGRN_VAL), lambda b, h: (0, 0)),
        ],
        out_specs=pl.BlockSpec((None, seq, LANES), lambda b, h: (b, 0, h)),
        out_shape=jax.ShapeDtypeStruct((batch, seq, D_MODEL), BF16),
        scratch_shapes=[pltpu.VMEM((seq, HGRN_VAL), F32)],
        compiler_params=pltpu.CompilerParams(
            dimension_semantics=("arbitrary", "arbitrary"), vmem_limit_bytes=VMEM_LIMIT),
        name="hgrn2",
    )(d4, d4, d4, a4, d4, hgrn_lb, norm_g)
    return out.reshape(batch * seq, D_MODEL)


def _merge_kernel(x_ref, lg_ref, lb_ref, o0_ref, l0_ref, o1_ref, l1_ref, o2_ref, l2_ref, rec_ref,
                  wg_ref, wa_ref, wr_ref, wo_ref, g1_ref, b1_ref, out_ref):
    tm = MERGE_TM

    def rows_of(ref, r):
        return jnp.concatenate([ref[c, r, :] for c in range(ATT_WIDTH // LANES)], axis=1)

    def attention(r):
        l0, l1, l2 = rows_of(l0_ref, r), rows_of(l1_ref, r), rows_of(l2_ref, r)
        mx = jnp.maximum(jnp.maximum(l0, l1), l2)
        e0, e1, e2 = jnp.exp(l0 - mx), jnp.exp(l1 - mx), jnp.exp(l2 - mx)
        return (e0 * rows_of(o0_ref, r) + e1 * rows_of(o1_ref, r) + e2 * rows_of(o2_ref, r)) / (e0 + e1 + e2)

    rs = tm // MERGE_SUB
    subs = [slice(s * rs, (s + 1) * rs) for s in range(MERGE_SUB)]
    ur = [_dot(rec_ref[r, :], wr_ref[...]) for r in subs]
    h = [_layer_norm(x_ref[r, :], lg_ref[...], lb_ref[...]) for r in subs]
    gates = [_dot(hs.astype(BF16), wg_ref[...]) for hs in h]
    ua = [_dot(attention(r).astype(BF16), wa_ref[...]) for r in subs]
    merged = [jax.nn.sigmoid(g[:, :D_MODEL]) * a + jax.nn.sigmoid(g[:, D_MODEL:]) * u
              for g, a, u in zip(gates, ua, ur)]
    z = [_dot(m.astype(BF16), wo_ref[...]) for m in merged]
    for r, hs, zs in zip(subs, h, z):
        out_ref[r, :] = _layer_norm(DEEPNORM_ALPHA * hs + zs, g1_ref[...], b1_ref[...])


def _const_spec(shape):
    return pl.BlockSpec(shape, lambda i: (0,) * len(shape))


def _merge(x2, lg, lb, att_groups, rec, wg, wa, wr, wo, g1, b1, batch, seq):
    t = x2.shape[0]
    tm = MERGE_TM
    att_spec = pl.BlockSpec((ATT_WIDTH // LANES, tm, LANES), lambda i: (0, i, 0))
    att_args = [a for pair in att_groups for a in pair]
    att_specs = [att_spec] * len(att_args)
    return pl.pallas_call(
        _merge_kernel,
        grid=(t // tm,),
        in_specs=[
            pl.BlockSpec((tm, D_MODEL), lambda i: (i, 0)),
            _const_spec((1, D_MODEL)), _const_spec((1, D_MODEL)),
            *att_specs,
            pl.BlockSpec((tm, D_MODEL), lambda i: (i, 0)),
            _const_spec(wg.shape), _const_spec(wa.shape), _const_spec(wr.shape), _const_spec(wo.shape),
            _const_spec((1, D_MODEL)), _const_spec((1, D_MODEL)),
        ],
        out_specs=pl.BlockSpec((tm, D_MODEL), lambda i: (i, 0)),
        out_shape=jax.ShapeDtypeStruct((t, D_MODEL), F32),
        compiler_params=pltpu.CompilerParams(
            dimension_semantics=("arbitrary",), vmem_limit_bytes=VMEM_LIMIT),
        name="merge_ln1",
    )(x2, lg, lb, *att_args, rec, wg, wa, wr, wo, g1, b1)


def _ffn_kernel(h_ref, wi_ref, wo_ref, g2_ref, b2_ref, out_ref, a_scr):
    h = h_ref[...]
    hb = h.astype(BF16)
    for c in range(D_FF // FFN_TC):
        gate = _dot(hb, wi_ref[:, c * FFN_TC:(c + 1) * FFN_TC])
        up = _dot(hb, wi_ref[:, D_FF + c * FFN_TC:D_FF + (c + 1) * FFN_TC])
        a_scr[:, c * FFN_TC:(c + 1) * FFN_TC] = (gate * jax.nn.sigmoid(gate) * up).astype(BF16)
    y = DEEPNORM_ALPHA * h + _dot(a_scr[...], wo_ref[...])
    out_ref[...] = _layer_norm(y, g2_ref[...], b2_ref[...])


def _ffn(h1, wi, wo, g2, b2):
    t = h1.shape[0]
    tm = FFN_TM
    return pl.pallas_call(
        _ffn_kernel,
        grid=(t // tm,),
        in_specs=[
            pl.BlockSpec((tm, D_MODEL), lambda i: (i, 0)),
            _const_spec(wi.shape), _const_spec(wo.shape),
            _const_spec((1, D_MODEL)), _const_spec((1, D_MODEL)),
        ],
        out_specs=pl.BlockSpec((tm, D_MODEL), lambda i: (i, 0)),
        out_shape=jax.ShapeDtypeStruct((t, D_MODEL), F32),
        scratch_shapes=[pltpu.VMEM((tm, D_FF), BF16)],
        compiler_params=pltpu.CompilerParams(
            dimension_semantics=("arbitrary",), vmem_limit_bytes=VMEM_LIMIT),
        name="ffn_ln2",
    )(h1, wi, wo, g2, b2)


def kernel(x, ln_in_g, ln_in_b, w_in, hgrn_lb, hgrn_norm_g, w_att_up, w_hgrn_up, w_o,
           ln1_g, ln1_b, w_ffn_in, w_ffn_out, ln2_g, ln2_b):
    batch, seq, d = x.shape
    assert d == D_MODEL and w_in.shape == (DEPTH, D_MODEL, IN_COLS)
    assert seq % PROJ_TM == 0 and seq % (16 * ATT_BK) == 0
    x2 = x.reshape(batch * seq, d)
    row = lambda v: v.reshape(1, -1).astype(F32)
    lg, lb = row(ln_in_g), row(ln_in_b)
    w_in_b = w_in[0].astype(BF16)

    proj_a, proj_b, proj_c, proj_d = _ln_proj(x2, lg, lb, w_in_b, batch, seq)
    att_groups = _dilated_attention(
        (proj_a.reshape(proj_a.shape[0], batch, 1, seq, PROJ_TN), proj_b, proj_c), batch, seq)
    rec = _hgrn(proj_a, proj_d, hgrn_lb.astype(F32), row(hgrn_norm_g[0]), batch, seq)
    h1 = _merge(x2, lg, lb, att_groups, rec, w_in_b[:, GATE_COL0:],
                w_att_up[0].astype(BF16), w_hgrn_up[0].astype(BF16), w_o[0].astype(BF16),
                row(ln1_g[0]), row(ln1_b[0]), batch, seq)
    out = _ffn(h1, w_ffn_in[0].astype(BF16), w_ffn_out[0].astype(BF16), row(ln2_g[0]), row(ln2_b[0]))
    return out.reshape(batch, seq, d)
```

```python
import functools

import jax
import jax.numpy as jnp
import numpy as np
from jax import lax
from jax.experimental import pallas as pl
from jax.experimental.pallas import tpu as pltpu

F32 = jnp.float32
BF16 = jnp.bfloat16

D_MODEL = 1024
DEPTH = 1
ATT_GROUPS = ((128, 1), (512, 4), (2048, 16))
N_GROUPS = len(ATT_GROUPS)
ATT_HEADS_PER_GROUP = 8
ATT_HEAD_DIM = 64
N_ATT_HEADS = N_GROUPS * ATT_HEADS_PER_GROUP
ATT_QKV = N_ATT_HEADS * ATT_HEAD_DIM
ATT_WIDTH = ATT_HEADS_PER_GROUP * ATT_HEAD_DIM
ALIBI_MAX_EXP = 8.0
NEG_INF = -1e30
HGRN_HEADS = 8
HGRN_KEY = 128
HGRN_VAL = 128
HGRN_CHUNK = 64
D_FF = 2816
DEEPNORM_ALPHA = (2.0 * DEPTH) ** 0.25
LN_EPS = 1e-5
RMS_EPS = 1e-6
QKV_COLS = 3 * ATT_QKV
REC_COLS = 3 * D_MODEL
GATE_COLS = 3 * D_MODEL
IN_COLS = QKV_COLS + REC_COLS + D_MODEL + GATE_COLS

LANES = 128
VMEM_LIMIT = 56 * 1024 * 1024
PROJ_TM = 2048
PROJ_TN = 512
PROJ_MM = 512
ATT_BQ = 128
ATT_BK = 256
ATT_HALF = 64
ATT_TS = 2048
MERGE_TM = 512
MERGE_SUB = 2
FFN_TM = 512
FFN_TC = 256


def _layer_norm(x, g, b):
    mu = jnp.mean(x, axis=-1, keepdims=True)
    xc = x - mu
    var = jnp.mean(xc * xc, axis=-1, keepdims=True)
    return xc * lax.rsqrt(var + LN_EPS) * g + b


def _dot(a, b):
    return jnp.dot(a, b, preferred_element_type=F32)


def _dot_nt(a, b):
    return lax.dot_general(a, b, (((1,), (1,)), ((), ())), preferred_element_type=F32)


def _dot_tn(a, b):
    return lax.dot_general(a, b, (((0,), (0,)), ((), ())), preferred_element_type=F32)


_PROJ_STEPS = (("D", 9), ("A", 0), ("D", 10), ("A", 3), ("D", 11), ("A", 6), ("D", 12), ("A", 15),
               ("D", 13), ("A", 16), ("D", 14), ("B", 1), ("D", 17), ("B", 4), ("D", 18), ("B", 7),
               ("C", 2), ("C", 5), ("C", 8))
_NJ = len(_PROJ_STEPS)
assert _PROJ_STEPS[0][0] == "D"
_KINDS = "ABCD"
_N_TILES = {kind: sum(k == kind for k, _ in _PROJ_STEPS) for kind in "ABCD"}
A_HI_TILE = 3
GATE_COL0 = QKV_COLS + REC_COLS + 2 * D_MODEL
_TAB_WCOL, _TAB_KIND, _TAB_BLOCK = 0, 1, 2


def _proj_tables():
    rows = [[w for _, w in _PROJ_STEPS], [_KINDS.index(k) for k, _ in _PROJ_STEPS]]
    for kind in _KINDS:
        n, blocks = 0, []
        for k, _ in _PROJ_STEPS:
            n += k == kind
            blocks.append(max(n - 1, 0))
        rows.append(blocks)
    return np.asarray(rows, np.int32)


def _ln_proj_kernel(tab_ref, x_ref, g_ref, b_ref, w_ref, a_ref, b_out_ref, c_out_ref, d_ref,
                    h_scr, hperm_scr, acc_scr):
    j = pl.program_id(1)
    kind = tab_ref[_TAB_KIND, j]
    d1, d2 = ATT_GROUPS[1][1], ATT_GROUPS[2][1]
    pblk = d2 * d2

    @pl.when(j == 0)
    def _():
        ri = lax.broadcasted_iota(jnp.int32, (pblk, pblk), 0)
        ci = lax.broadcasted_iota(jnp.int32, (pblk, pblk), 1)
        perm = (ci == (ri % d2) * d2 + ri // d2).astype(BF16)
        per_mm = PROJ_MM // pblk
        hb = []
        for c in range(PROJ_TM // pblk + 1):
            if c < PROJ_TM // pblk:
                r = slice(c * pblk, (c + 1) * pblk)
                hb.append(_layer_norm(x_ref[r, :], g_ref[...], b_ref[...]).astype(BF16))
                h_scr[r, :] = hb[c]
            if c >= 1:
                hperm_scr[(c - 1) * pblk:c * pblk, :] = _dot(perm, hb[c - 1]).astype(BF16)
            if c >= 1 and c % per_mm == 0:
                rc = slice((c - per_mm) * pblk, c * pblk)
                d_ref[rc, :] = _dot(jnp.concatenate(hb[c - per_mm:c], axis=0), w_ref[...])

    row_chunks = [slice(c * PROJ_MM, (c + 1) * PROJ_MM) for c in range(PROJ_TM // PROJ_MM)]

    @pl.when(kind == _KINDS.index("A"))
    def _():
        for rc in row_chunks:
            a_ref[rc, :] = _dot(h_scr[rc, :], w_ref[...]).astype(BF16)

    @pl.when(kind == _KINDS.index("B"))
    def _():
        n = PROJ_TM // d1
        for rc in row_chunks:
            acc = _dot(h_scr[rc, :], w_ref[...])
            for c in range(PROJ_TN // LANES):
                acc_scr[c, rc, :] = acc[:, c * LANES:(c + 1) * LANES]
        for c in range(PROJ_TN // LANES):
            for r in range(d1):
                b_out_ref[r, :, c * LANES:(c + 1) * LANES] = (
                    acc_scr[c, pl.ds(r, n, stride=d1), :].astype(BF16))

    @pl.when(kind == _KINDS.index("C"))
    def _():
        for ci, rc in enumerate(row_chunks):
            acc = _dot(hperm_scr[rc, :], w_ref[...]).astype(BF16)
            for b in range(PROJ_MM // pblk):
                blk = ci * (PROJ_MM // pblk) + b
                for r in range(d2):
                    src = b * pblk + r * d2
                    c_out_ref[r, blk * d2:(blk + 1) * d2, :] = acc[src:src + d2, :]

    @pl.when((kind == _KINDS.index("D")) & (j > 0))
    def _():
        for rc in row_chunks:
            d_ref[rc, :] = _dot(h_scr[rc, :], w_ref[...])


def _ln_proj(x2, g, b, w_in, batch, seq):
    t = x2.shape[0]
    tpb = seq // PROJ_TM
    d1, d2 = ATT_GROUPS[1][1], ATT_GROUPS[2][1]

    def block(tab, kind, j):
        return tab[_TAB_BLOCK + _KINDS.index(kind), j]

    def token_spec(kind):
        return pl.BlockSpec((None, PROJ_TM, PROJ_TN), lambda i, j, tab: (block(tab, kind, j), i, 0))

    def stream_spec(dil, kind):
        return pl.BlockSpec((None, None, dil, PROJ_TM // dil, PROJ_TN),
                            lambda i, j, tab: (block(tab, kind, j), i // tpb, 0, i % tpb, 0))

    grid_spec = pltpu.PrefetchScalarGridSpec(
        num_scalar_prefetch=1,
        grid=(t // PROJ_TM, _NJ),
        in_specs=[
            pl.BlockSpec((PROJ_TM, D_MODEL), lambda i, j, tab: (i, 0)),
            pl.BlockSpec((1, D_MODEL), lambda i, j, tab: (0, 0)),
            pl.BlockSpec((1, D_MODEL), lambda i, j, tab: (0, 0)),
            pl.BlockSpec((D_MODEL, PROJ_TN), lambda i, j, tab: (0, tab[_TAB_WCOL, j])),
        ],
        out_specs=[
            token_spec("A"),
            stream_spec(d1, "B"),
            stream_spec(d2, "C"),
            token_spec("D"),
        ],
        scratch_shapes=[pltpu.VMEM((PROJ_TM, D_MODEL), BF16),
                        pltpu.VMEM((PROJ_TM, D_MODEL), BF16),
                        pltpu.VMEM((PROJ_TN // LANES, PROJ_TM, LANES), F32)],
    )
    return pl.pallas_call(
        _ln_proj_kernel,
        grid_spec=grid_spec,
        out_shape=[
            jax.ShapeDtypeStruct((_N_TILES["A"], t, PROJ_TN), BF16),
            jax.ShapeDtypeStruct((_N_TILES["B"], batch, d1, seq // d1, PROJ_TN), BF16),
            jax.ShapeDtypeStruct((_N_TILES["C"], batch, d2, seq // d2, PROJ_TN), BF16),
            jax.ShapeDtypeStruct((_N_TILES["D"], t, PROJ_TN), F32),
        ],
        compiler_params=pltpu.CompilerParams(
            dimension_semantics=("arbitrary", "arbitrary"), vmem_limit_bytes=VMEM_LIMIT),
        name="ln_proj",
    )(jnp.asarray(_proj_tables()), x2, g, b, w_in)


def _attn_bias_tables(slopes, dil):
    ql = jnp.arange(ATT_BQ)[:, None]
    kl = jnp.arange(ATT_BK)[None, :]
    tabs = []
    for off in (0, -ATT_HALF, -2 * ATT_HALF):
        rel = kl - ql + off
        valid = jnp.abs(rel) <= ATT_HALF
        dist = (dil * jnp.abs(rel)).astype(F32)
        bias = -slopes.astype(F32)[:, None, None] * dist[None]
        tabs.append(jnp.where(valid[None], bias, NEG_INF))
    return jnp.stack(tabs)


def _attn_kernel(q_ref, k_ref, v_ref, tab_ref, o_ref, lse_ref, *, m, dil, bps):
    tile = pl.program_id(1)
    nblk = m // ATT_BQ
    lane = lax.broadcasted_iota(jnp.int32, (1, LANES), 1)
    lo = lane < ATT_HEAD_DIM
    scale = ATT_HEAD_DIM ** -0.5
    qmask = (jnp.where(lo, scale, 0.0).astype(BF16), jnp.where(lo, 0.0, scale).astype(BF16))

    def block(r, i):
        blk = tile * bps + i
        k0 = pl.multiple_of(jnp.clip(blk * ATT_BQ - ATT_HALF, 0, m - ATT_BK), ATT_HALF)
        var = jnp.where(blk == 0, 0, jnp.where(blk == nblk - 1, 2, 1))
        rq = slice(i * ATT_BQ, (i + 1) * ATT_BQ)
        ro = pl.ds(i * ATT_BQ * dil + r, ATT_BQ, stride=dil) if dil > 1 else rq
        return dict(r=r, rq=rq, rk=pl.ds(k0, ATT_BK), ro=ro, var=var)

    def stage_scores(t):
        b, cs = t["blk"], t["cols"]
        q2, k2 = q_ref[b["r"], b["rq"], cs], k_ref[b["r"], b["rk"], cs]
        t["s"] = [_dot_nt(q2 * qmask[sub], k2) + tab_ref[b["var"], 2 * t["pair"] + sub] for sub in range(2)]

    def stage_softmax(t):
        t["p"], t["l"], t["mx"] = [], [], []
        for s in t.pop("s"):
            mx = jnp.max(s, axis=-1, keepdims=True)
            p = jnp.exp(s - mx)
            t["mx"].append(mx)
            t["l"].append(jnp.sum(p, axis=-1, keepdims=True))
            t["p"].append(p.astype(BF16))

    def stage_values(t):
        b = t["blk"]
        v2 = v_ref[b["r"], b["rk"], t["cols"]]
        pv = jnp.where(lo, _dot(t["p"][0], v2), _dot(t["p"][1], v2))
        l = jnp.where(lo, t["l"][0], t["l"][1])
        o_ref[t["pair"], b["ro"], :] = pv / l
        lse_ref[t["pair"], b["ro"], :] = jnp.where(lo, t["mx"][0], t["mx"][1]) + jnp.log(l)
        t.clear()

    stages = ((stage_scores, 0), (stage_softmax, 1), (stage_values, 3))
    tasks = []
    for r in range(dil):
        for i in range(bps):
            b = block(r, i)
            tasks += [dict(blk=b, pair=pair, cols=slice(pair * LANES, (pair + 1) * LANES))
                      for pair in range(ATT_HEADS_PER_GROUP // 2)]
    for slot in range(len(tasks) + stages[-1][1]):
        for stage, lag in stages:
            if 0 <= slot - lag < len(tasks):
                stage(tasks[slot - lag])


def _attn_group(qkv5, tab, g, dil, batch, seq):
    m = seq // dil
    ts = ATT_TS
    p = ts // dil
    tpb = seq // ts
    nlb = ATT_WIDTH // LANES
    q_spec = pl.BlockSpec((None, None, dil, p, ATT_WIDTH), lambda b, i: (0, b, 0, i, 0))
    k_spec = pl.BlockSpec((None, None, dil, m, ATT_WIDTH), lambda b, i: (1, b, 0, 0, 0))
    v_spec = pl.BlockSpec((None, None, dil, m, ATT_WIDTH), lambda b, i: (2, b, 0, 0, 0))
    o_spec = pl.BlockSpec((nlb, ts, LANES), lambda b, i: (0, b * tpb + i, 0))
    tab_spec = pl.BlockSpec(tab.shape, lambda b, i: (0, 0, 0, 0))
    out_shape = jax.ShapeDtypeStruct((nlb, batch * seq, LANES), F32)
    return pl.pallas_call(
        functools.partial(_attn_kernel, m=m, dil=dil, bps=p // ATT_BQ),
        grid=(batch, tpb),
        in_specs=[q_spec, k_spec, v_spec, tab_spec],
        out_specs=[o_spec, o_spec],
        out_shape=[out_shape, out_shape],
        compiler_params=pltpu.CompilerParams(
            dimension_semantics=("arbitrary", "arbitrary"), vmem_limit_bytes=VMEM_LIMIT),
        name=f"dilated_attn_g{g}",
    )(qkv5, qkv5, qkv5, tab)


def _dilated_attention(qkv_groups, batch, seq):
    slopes = 2.0 ** (-ALIBI_MAX_EXP * jnp.arange(1, N_ATT_HEADS + 1, dtype=F32) / N_ATT_HEADS)
    slopes = slopes.reshape(N_GROUPS, ATT_HEADS_PER_GROUP)
    outs = []
    for g, (window, dil) in enumerate(ATT_GROUPS):
        assert window // (2 * dil) == ATT_HALF
        outs.append(_attn_group(qkv_groups[g], _attn_bias_tables(slopes[g], dil), g, dil, batch, seq))
    return outs


def _split3(x):
    hi = x.astype(BF16)
    r1 = x - hi.astype(F32)
    mid = r1.astype(BF16)
    lo = (r1 - mid.astype(F32)).astype(BF16)
    return hi, mid, lo


def _hgrn_kernel(hq_ref, hf_ref, hb_ref, hi_ref, hg_ref, lbp_ref, ng_ref, out_ref, o_scr, *, seq):
    c = HGRN_CHUNK
    n_chunks = seq // c
    row = lax.broadcasted_iota(jnp.int32, (c, c), 0)
    col = lax.broadcasted_iota(jnp.int32, (c, c), 1)
    tri = (row >= col).astype(BF16)
    tri3 = jnp.concatenate([tri, tri, tri], axis=1)
    causal = (row >= col, row <= col)
    qscale = HGRN_KEY ** -0.5

    def lower_bound(d):
        a = lbp_ref[d]
        mx = jnp.max(a, axis=0, keepdims=True)
        e = jnp.exp(a - mx)
        return e[0:1] / jnp.sum(e, axis=0, keepdims=True)

    lbs = (lower_bound(0), lower_bound(1))
    q_cache = {}

    def stage_elementwise(d, t):
        r, g = t["rows"], t["chunk"]
        if g in q_cache:
            q = q_cache.pop(g)
        else:
            xq = hq_ref[r, :]
            q = q_cache[g] = xq * jax.nn.sigmoid(xq) * qscale
        lb = lbs[d]
        f = lb + (1.0 - lb) * jax.nn.sigmoid((hf_ref, hb_ref)[d][r, :])
        t.update(q=q, kk=1.0 - f, lf=jnp.log2(f), v=hi_ref[r, :])

    def stage_prefix(d, t):
        hi, mid, lo = _split3(t["lf"])
        t["pre"] = _dot(tri3, jnp.concatenate([hi, mid, lo], axis=0))

    def stage_decay(d, t):
        pre = t["pre"]
        tot = pre[c - 1:c]
        if d == 0:
            cum, cl = pre, tot
        else:
            cum = tot - pre + t["lf"]
            cl = cum[0:1]
        ec, ecl = jnp.exp2(cum), jnp.exp2(cl)
        kinv = t["kk"] / ec
        t.update(qd=(t["q"] * ec).astype(BF16), ki=kinv.astype(BF16),
                 kd=(kinv * ecl).astype(BF16), dec=ecl)

    def stage_local_dots(d, t):
        t["a"] = _dot_nt(t["qd"], t["ki"])
        t["ut"] = _dot_tn(t["v"], t["kd"])

    def stage_state(d, t, st):
        t["entry"] = st.astype(BF16)
        return st * t["dec"] + t["ut"]

    def stage_output(d, t):
        r, g = t["rows"], t["chunk"]
        o = _dot(jnp.where(causal[d], t["a"], 0.0).astype(BF16), t["v"]) + _dot_nt(t["qd"], t["entry"])
        fwd_first = g < n_chunks - 1 - g
        if (d == 0) == fwd_first:
            o_scr[r, :] = o
        else:
            o = o_scr[r, :] + o
            o = o * lax.rsqrt(jnp.mean(o * o, axis=-1, keepdims=True) + RMS_EPS) * ng_ref[...]
            xg = hg_ref[r, :]
            out_ref[r, :] = (o * (xg * jax.nn.sigmoid(xg))).astype(BF16)

    stages = (stage_elementwise, stage_prefix, stage_decay, stage_local_dots, None, stage_output)

    order = (list(range(n_chunks)), list(reversed(range(n_chunks))))
    tasks = [[{"chunk": g, "rows": slice(g * c, (g + 1) * c)} for g in order[d]] for d in (0, 1)]
    st = [jnp.zeros((HGRN_VAL, HGRN_KEY), F32), jnp.zeros((HGRN_VAL, HGRN_KEY), F32)]
    for slot in range(n_chunks + len(stages) - 1):
        for k, stage in enumerate(stages):
            i = slot - k
            if 0 <= i < n_chunks:
                for d in (0, 1):
                    if stage is None:
                        st[d] = stage_state(d, tasks[d][i], st[d])
                    else:
                        stage(d, tasks[d][i])
                        if stage is stage_output:
                            tasks[d][i].clear()


def _hgrn(proj_a, proj_d, hgrn_lb, norm_g, batch, seq):
    a4 = proj_a.reshape(proj_a.shape[0], batch, seq, PROJ_TN)
    d4 = proj_d.reshape(proj_d.shape[0], batch, seq, PROJ_TN)
    nh = HGRN_HEADS
    hpt = PROJ_TN // LANES
    tpf = D_MODEL // PROJ_TN

    def col(field, tile0=0):
        return pl.BlockSpec((None, None, seq, LANES),
                            lambda b, h: (tile0 + field * tpf + h // hpt, b, 0, h % hpt))

    out = pl.pallas_call(
        functools.partial(_hgrn_kernel, seq=seq),
        grid=(batch, nh),
        in_specs=[
            col(0), col(1), col(2),
            col(0, A_HI_TILE),
            col(3),
            pl.BlockSpec((2, DEPTH + 1, LANES), lambda b, h: (0, 0, h)),
            pl.BlockSpec((1, HGRN_VAL), lambda b, h: (0, 0)),
        ],
        out_specs=pl.BlockSpec((None, seq, LANES), lambda b, h: (b, 0, h)),
        out_shape=jax.ShapeDtypeStruct((batch, seq, D_MODEL), BF16),
        scratch_shapes=[pltpu.VMEM((seq, HGRN_VAL), F32)],
        compiler_params=pltpu.CompilerParams(
            dimension_semantics=("arbitrary", "arbitrary"), vmem_limit_bytes=VMEM_LIMIT),
        name="hgrn2",
    )(d4, d4, d4, a4, d4, hgrn_lb, norm_g)
    return out.reshape(batch * seq, D_MODEL)


def _merge_kernel(x_ref, lg_ref, lb_ref, o0_ref, l0_ref, o1_ref, l1_ref, o2_ref, l2_ref, rec_ref,
                  wg_ref, wa_ref, wr_ref, wo_ref, g1_ref, b1_ref, out_ref):
    tm = MERGE_TM

    def rows_of(ref, r):
        return jnp.concatenate([ref[c, r, :] for c in range(ATT_WIDTH // LANES)], axis=1)

    def attention(r):
        l0, l1, l2 = rows_of(l0_ref, r), rows_of(l1_ref, r), rows_of(l2_ref, r)
        mx = jnp.maximum(jnp.maximum(l0, l1), l2)
        e0, e1, e2 = jnp.exp(l0 - mx), jnp.exp(l1 - mx), jnp.exp(l2 - mx)
        return (e0 * rows_of(o0_ref, r) + e1 * rows_of(o1_ref, r) + e2 * rows_of(o2_ref, r)) / (e0 + e1 + e2)

    rs = tm // MERGE_SUB
    subs = [slice(s * rs, (s + 1) * rs) for s in range(MERGE_SUB)]
    ur = [_dot(rec_ref[r, :], wr_ref[...]) for r in subs]
    h = [_layer_norm(x_ref[r, :], lg_ref[...], lb_ref[...]) for r in subs]
    gates = [_dot(hs.astype(BF16), wg_ref[...]) for hs in h]
    ua = [_dot(attention(r).astype(BF16), wa_ref[...]) for r in subs]
    merged = [jax.nn.sigmoid(g[:, :D_MODEL]) * a + jax.nn.sigmoid(g[:, D_MODEL:]) * u
              for g, a, u in zip(gates, ua, ur)]
    z = [_dot(m.astype(BF16), wo_ref[...]) for m in merged]
    for r, hs, zs in zip(subs, h, z):
        out_ref[r, :] = _layer_norm(DEEPNORM_ALPHA * hs + zs, g1_ref[...], b1_ref[...])


def _const_spec(shape):
    return pl.BlockSpec(shape, lambda i: (0,) * len(shape))


def _merge(x2, lg, lb, att_groups, rec, wg, wa, wr, wo, g1, b1, batch, seq):
    t = x2.shape[0]
    tm = MERGE_TM
    att_spec = pl.BlockSpec((ATT_WIDTH // LANES, tm, LANES), lambda i: (0, i, 0))
    att_args = [a for pair in att_groups for a in pair]
    att_specs = [att_spec] * len(att_args)
    return pl.pallas_call(
        _merge_kernel,
        grid=(t // tm,),
        in_specs=[
            pl.BlockSpec((tm, D_MODEL), lambda i: (i, 0)),
            _const_spec((1, D_MODEL)), _const_spec((1, D_MODEL)),
            *att_specs,
            pl.BlockSpec((tm, D_MODEL), lambda i: (i, 0)),
            _const_spec(wg.shape), _const_spec(wa.shape), _const_spec(wr.shape), _const_spec(wo.shape),
            _const_spec((1, D_MODEL)), _const_spec((1, D_MODEL)),
        ],
        out_specs=pl.BlockSpec((tm, D_MODEL), lambda i: (i, 0)),
        out_shape=jax.ShapeDtypeStruct((t, D_MODEL), F32),
        compiler_params=pltpu.CompilerParams(
            dimension_semantics=("arbitrary",), vmem_limit_bytes=VMEM_LIMIT),
        name="merge_ln1",
    )(x2, lg, lb, *att_args, rec, wg, wa, wr, wo, g1, b1)


def _ffn_kernel(h_ref, wi_ref, wo_ref, g2_ref, b2_ref, out_ref, a_scr):
    h = h_ref[...]
    hb = h.astype(BF16)
    for c in range(D_FF // FFN_TC):
        gate = _dot(hb, wi_ref[:, c * FFN_TC:(c + 1) * FFN_TC])
        up = _dot(hb, wi_ref[:, D_FF + c * FFN_TC:D_FF + (c + 1) * FFN_TC])
        a_scr[:, c * FFN_TC:(c + 1) * FFN_TC] = (gate * jax.nn.sigmoid(gate) * up).astype(BF16)
    y = DEEPNORM_ALPHA * h + _dot(a_scr[...], wo_ref[...])
    out_ref[...] = _layer_norm(y, g2_ref[...], b2_ref[...])


def _ffn(h1, wi, wo, g2, b2):
    t = h1.shape[0]
    tm = FFN_TM
    return pl.pallas_call(
        _ffn_kernel,
        grid=(t // tm,),
        in_specs=[
            pl.BlockSpec((tm, D_MODEL), lambda i: (i, 0)),
            _const_spec(wi.shape), _const_spec(wo.shape),
            _const_spec((1, D_MODEL)), _const_spec((1, D_MODEL)),
        ],
        out_specs=pl.BlockSpec((tm, D_MODEL), lambda i: (i, 0)),
        out_shape=jax.ShapeDtypeStruct((t, D_MODEL), F32),
        scratch_shapes=[pltpu.VMEM((tm, D_FF), BF16)],
        compiler_params=pltpu.CompilerParams(
            dimension_semantics=("arbitrary",), vmem_limit_bytes=VMEM_LIMIT),
        name="ffn_ln2",
    )(h1, wi, wo, g2, b2)


def kernel(x, ln_in_g, ln_in_b, w_in, hgrn_lb, hgrn_norm_g, w_att_up, w_hgrn_up, w_o,
           ln1_g, ln1_b, w_ffn_in, w_ffn_out, ln2_g, ln2_b):
    batch, seq, d = x.shape
    assert d == D_MODEL and w_in.shape == (DEPTH, D_MODEL, IN_COLS)
    assert seq % PROJ_TM == 0 and seq % (16 * ATT_BK) == 0
    x2 = x.reshape(batch * seq, d)
    row = lambda v: v.reshape(1, -1).astype(F32)
    lg, lb = row(ln_in_g), row(ln_in_b)
    w_in_b = w_in[0].astype(BF16)

    proj_a, proj_b, proj_c, proj_d = _ln_proj(x2, lg, lb, w_in_b, batch, seq)
    att_groups = _dilated_attention(
        (proj_a.reshape(proj_a.shape[0], batch, 1, seq, PROJ_TN), proj_b, proj_c), batch, seq)
    rec = _hgrn(proj_a, proj_d, hgrn_lb.astype(F32), row(hgrn_norm_g[0]), batch, seq)
    h1 = _merge(x2, lg, lb, att_groups, rec, w_in_b[:, GATE_COL0:],
                w_att_up[0].astype(BF16), w_hgrn_up[0].astype(BF16), w_o[0].astype(BF16),
                row(ln1_g[0]), row(ln1_b[0]), batch, seq)
    out = _ffn(h1, w_ffn_in[0].astype(BF16), w_ffn_out[0].astype(BF16), row(ln2_g[0]), row(ln2_b[0]))
    return out.reshape(batch, seq, d)
```

```python
import functools

import jax
import jax.numpy as jnp
import numpy as np
from jax import lax
from jax.experimental import pallas as pl
from jax.experimental.pallas import tpu as pltpu

F32 = jnp.float32
BF16 = jnp.bfloat16

D_MODEL = 1024
DEPTH = 1
ATT_GROUPS = ((128, 1), (512, 4), (2048, 16))
N_GROUPS = len(ATT_GROUPS)
ATT_HEADS_PER_GROUP = 8
ATT_HEAD_DIM = 64
N_ATT_HEADS = N_GROUPS * ATT_HEADS_PER_GROUP
ATT_QKV = N_ATT_HEADS * ATT_HEAD_DIM
ATT_WIDTH = ATT_HEADS_PER_GROUP * ATT_HEAD_DIM
ALIBI_MAX_EXP = 8.0
NEG_INF = -1e30
HGRN_HEADS = 8
HGRN_KEY = 128
HGRN_VAL = 128
HGRN_CHUNK = 64
D_FF = 2816
DEEPNORM_ALPHA = (2.0 * DEPTH) ** 0.25
LN_EPS = 1e-5
RMS_EPS = 1e-6
QKV_COLS = 3 * ATT_QKV
REC_COLS = 3 * D_MODEL
GATE_COLS = 3 * D_MODEL
IN_COLS = QKV_COLS + REC_COLS + D_MODEL + GATE_COLS

LANES = 128
VMEM_LIMIT = 56 * 1024 * 1024
PROJ_TM = 2048
PROJ_TN = 512
PROJ_MM = 512
ATT_BQ = 128
ATT_BK = 256
ATT_HALF = 64
ATT_TS = 2048
MERGE_TM = 512
MERGE_SUB = 2
FFN_TM = 1024
FFN_MM = 512
FFN_TC = 256


def _layer_norm(x, g, b):
    mu = jnp.mean(x, axis=-1, keepdims=True)
    xc = x - mu
    var = jnp.mean(xc * xc, axis=-1, keepdims=True)
    return xc * lax.rsqrt(var + LN_EPS) * g + b


def _dot(a, b):
    return jnp.dot(a, b, preferred_element_type=F32)


def _dot_nt(a, b):
    return lax.dot_general(a, b, (((1,), (1,)), ((), ())), preferred_element_type=F32)


def _dot_tn(a, b):
    return lax.dot_general(a, b, (((0,), (0,)), ((), ())), preferred_element_type=F32)


_PROJ_STEPS = (("D", 9), ("A", 0), ("D", 10), ("A", 3), ("D", 11), ("A", 6), ("D", 12), ("A", 15),
               ("D", 13), ("A", 16), ("D", 14), ("B", 1), ("D", 17), ("B", 4), ("D", 18), ("B", 7),
               ("C", 2), ("C", 5), ("C", 8))
_NJ = len(_PROJ_STEPS)
assert _PROJ_STEPS[0][0] == "D"
_KINDS = "ABCD"
_N_TILES = {kind: sum(k == kind for k, _ in _PROJ_STEPS) for kind in "ABCD"}
A_HI_TILE = 3
GATE_COL0 = QKV_COLS + REC_COLS + 2 * D_MODEL
_TAB_WCOL, _TAB_KIND, _TAB_BLOCK = 0, 1, 2


def _proj_tables():
    rows = [[w for _, w in _PROJ_STEPS], [_KINDS.index(k) for k, _ in _PROJ_STEPS]]
    for kind in _KINDS:
        n, blocks = 0, []
        for k, _ in _PROJ_STEPS:
            n += k == kind
            blocks.append(max(n - 1, 0))
        rows.append(blocks)
    return np.asarray(rows, np.int32)


def _ln_proj_kernel(tab_ref, x_ref, g_ref, b_ref, w_ref, a_ref, b_out_ref, c_out_ref, d_ref,
                    h_scr, hperm_scr, acc_scr):
    j = pl.program_id(1)
    kind = tab_ref[_TAB_KIND, j]
    d1, d2 = ATT_GROUPS[1][1], ATT_GROUPS[2][1]
    pblk = d2 * d2

    @pl.when(j == 0)
    def _():
        ri = lax.broadcasted_iota(jnp.int32, (pblk, pblk), 0)
        ci = lax.broadcasted_iota(jnp.int32, (pblk, pblk), 1)
        perm = (ci == (ri % d2) * d2 + ri // d2).astype(BF16)
        per_mm = PROJ_MM // pblk
        hb = []
        for c in range(PROJ_TM // pblk + 1):
            if c < PROJ_TM // pblk:
                r = slice(c * pblk, (c + 1) * pblk)
                hb.append(_layer_norm(x_ref[r, :], g_ref[...], b_ref[...]).astype(BF16))
                h_scr[r, :] = hb[c]
            if c >= 1:
                hperm_scr[(c - 1) * pblk:c * pblk, :] = _dot(perm, hb[c - 1]).astype(BF16)
            if c >= 1 and c % per_mm == 0:
                rc = slice((c - per_mm) * pblk, c * pblk)
                d_ref[rc, :] = _dot(jnp.concatenate(hb[c - per_mm:c], axis=0), w_ref[...])

    row_chunks = [slice(c * PROJ_MM, (c + 1) * PROJ_MM) for c in range(PROJ_TM // PROJ_MM)]

    @pl.when(kind == _KINDS.index("A"))
    def _():
        for rc in row_chunks:
            a_ref[rc, :] = _dot(h_scr[rc, :], w_ref[...]).astype(BF16)

    @pl.when(kind == _KINDS.index("B"))
    def _():
        n = PROJ_TM // d1
        for rc in row_chunks:
            acc = _dot(h_scr[rc, :], w_ref[...])
            for c in range(PROJ_TN // LANES):
                acc_scr[c, rc, :] = acc[:, c * LANES:(c + 1) * LANES]
        for c in range(PROJ_TN // LANES):
            for r in range(d1):
                b_out_ref[r, :, c * LANES:(c + 1) * LANES] = (
                    acc_scr[c, pl.ds(r, n, stride=d1), :].astype(BF16))

    @pl.when(kind == _KINDS.index("C"))
    def _():
        for ci, rc in enumerate(row_chunks):
            acc = _dot(hperm_scr[rc, :], w_ref[...]).astype(BF16)
            for b in range(PROJ_MM // pblk):
                blk = ci * (PROJ_MM // pblk) + b
                for r in range(d2):
                    src = b * pblk + r * d2
                    c_out_ref[r, blk * d2:(blk + 1) * d2, :] = acc[src:src + d2, :]

    @pl.when((kind == _KINDS.index("D")) & (j > 0))
    def _():
        for rc in row_chunks:
            d_ref[rc, :] = _dot(h_scr[rc, :], w_ref[...])


def _ln_proj(x2, g, b, w_in, batch, seq):
    t = x2.shape[0]
    tpb = seq // PROJ_TM
    d1, d2 = ATT_GROUPS[1][1], ATT_GROUPS[2][1]

    def block(tab, kind, j):
        return tab[_TAB_BLOCK + _KINDS.index(kind), j]

    def token_spec(kind):
        return pl.BlockSpec((None, PROJ_TM, PROJ_TN), lambda i, j, tab: (block(tab, kind, j), i, 0))

    def stream_spec(dil, kind):
        return pl.BlockSpec((None, None, dil, PROJ_TM // dil, PROJ_TN),
                            lambda i, j, tab: (block(tab, kind, j), i // tpb, 0, i % tpb, 0))

    grid_spec = pltpu.PrefetchScalarGridSpec(
        num_scalar_prefetch=1,
        grid=(t // PROJ_TM, _NJ),
        in_specs=[
            pl.BlockSpec((PROJ_TM, D_MODEL), lambda i, j, tab: (i, 0)),
            pl.BlockSpec((1, D_MODEL), lambda i, j, tab: (0, 0)),
            pl.BlockSpec((1, D_MODEL), lambda i, j, tab: (0, 0)),
            pl.BlockSpec((D_MODEL, PROJ_TN), lambda i, j, tab: (0, tab[_TAB_WCOL, j])),
        ],
        out_specs=[
            token_spec("A"),
            stream_spec(d1, "B"),
            stream_spec(d2, "C"),
            token_spec("D"),
        ],
        scratch_shapes=[pltpu.VMEM((PROJ_TM, D_MODEL), BF16),
                        pltpu.VMEM((PROJ_TM, D_MODEL), BF16),
                        pltpu.VMEM((PROJ_TN // LANES, PROJ_TM, LANES), F32)],
    )
    return pl.pallas_call(
        _ln_proj_kernel,
        grid_spec=grid_spec,
        out_shape=[
            jax.ShapeDtypeStruct((_N_TILES["A"], t, PROJ_TN), BF16),
            jax.ShapeDtypeStruct((_N_TILES["B"], batch, d1, seq // d1, PROJ_TN), BF16),
            jax.ShapeDtypeStruct((_N_TILES["C"], batch, d2, seq // d2, PROJ_TN), BF16),
            jax.ShapeDtypeStruct((_N_TILES["D"], t, PROJ_TN), F32),
        ],
        compiler_params=pltpu.CompilerParams(
            dimension_semantics=("arbitrary", "arbitrary"), vmem_limit_bytes=VMEM_LIMIT),
        name="ln_proj",
    )(jnp.asarray(_proj_tables()), x2, g, b, w_in)


def _attn_bias_tables(slopes, dil):
    ql = jnp.arange(ATT_BQ)[:, None]
    kl = jnp.arange(ATT_BK)[None, :]
    tabs = []
    for off in (0, -ATT_HALF, -2 * ATT_HALF):
        rel = kl - ql + off
        valid = jnp.abs(rel) <= ATT_HALF
        dist = (dil * jnp.abs(rel)).astype(F32)
        bias = -slopes.astype(F32)[:, None, None] * dist[None]
        tabs.append(jnp.where(valid[None], bias, NEG_INF))
    return jnp.stack(tabs)


def _attn_kernel(q_ref, k_ref, v_ref, tab_ref, o_ref, lse_ref, *, m, dil, bps):
    tile = pl.program_id(1)
    nblk = m // ATT_BQ
    lane = lax.broadcasted_iota(jnp.int32, (1, LANES), 1)
    lo = lane < ATT_HEAD_DIM
    scale = ATT_HEAD_DIM ** -0.5
    qmask = (jnp.where(lo, scale, 0.0).astype(BF16), jnp.where(lo, 0.0, scale).astype(BF16))

    def block(r, i):
        blk = tile * bps + i
        k0 = pl.multiple_of(jnp.clip(blk * ATT_BQ - ATT_HALF, 0, m - ATT_BK), ATT_HALF)
        var = jnp.where(blk == 0, 0, jnp.where(blk == nblk - 1, 2, 1))
        rq = slice(i * ATT_BQ, (i + 1) * ATT_BQ)
        ro = pl.ds(i * ATT_BQ * dil + r, ATT_BQ, stride=dil) if dil > 1 else rq
        return dict(r=r, rq=rq, rk=pl.ds(k0, ATT_BK), ro=ro, var=var)

    def stage_scores(t):
        b, cs = t["blk"], t["cols"]
        q2, k2 = q_ref[b["r"], b["rq"], cs], k_ref[b["r"], b["rk"], cs]
        t["s"] = [_dot_nt(q2 * qmask[sub], k2) + tab_ref[b["var"], 2 * t["pair"] + sub] for sub in range(2)]

    def stage_softmax(t):
        t["p"], t["l"], t["mx"] = [], [], []
        for s in t.pop("s"):
            mx = jnp.max(s, axis=-1, keepdims=True)
            p = jnp.exp(s - mx)
            t["mx"].append(mx)
            t["l"].append(jnp.sum(p, axis=-1, keepdims=True))
            t["p"].append(p.astype(BF16))

    def stage_values(t):
        b = t["blk"]
        v2 = v_ref[b["r"], b["rk"], t["cols"]]
        pv = jnp.where(lo, _dot(t["p"][0], v2), _dot(t["p"][1], v2))
        l = jnp.where(lo, t["l"][0], t["l"][1])
        o_ref[t["pair"], b["ro"], :] = pv / l
        lse_ref[t["pair"], b["ro"], :] = jnp.where(lo, t["mx"][0], t["mx"][1]) + jnp.log(l)
        t.clear()

    stages = ((stage_scores, 0), (stage_softmax, 1), (stage_values, 3))
    tasks = []
    for r in range(dil):
        for i in range(bps):
            b = block(r, i)
            tasks += [dict(blk=b, pair=pair, cols=slice(pair * LANES, (pair + 1) * LANES))
                      for pair in range(ATT_HEADS_PER_GROUP // 2)]
    for slot in range(len(tasks) + stages[-1][1]):
        for stage, lag in stages:
            if 0 <= slot - lag < len(tasks):
                stage(tasks[slot - lag])


def _attn_group(qkv5, tab, g, dil, batch, seq):
    m = seq // dil
    ts = ATT_TS
    p = ts // dil
    tpb = seq // ts
    nlb = ATT_WIDTH // LANES
    q_spec = pl.BlockSpec((None, None, dil, p, ATT_WIDTH), lambda b, i: (0, b, 0, i, 0))
    k_spec = pl.BlockSpec((None, None, dil, m, ATT_WIDTH), lambda b, i: (1, b, 0, 0, 0))
    v_spec = pl.BlockSpec((None, None, dil, m, ATT_WIDTH), lambda b, i: (2, b, 0, 0, 0))
    o_spec = pl.BlockSpec((nlb, ts, LANES), lambda b, i: (0, b * tpb + i, 0))
    tab_spec = pl.BlockSpec(tab.shape, lambda b, i: (0, 0, 0, 0))
    out_shape = jax.ShapeDtypeStruct((nlb, batch * seq, LANES), F32)
    return pl.pallas_call(
        functools.partial(_attn_kernel, m=m, dil=dil, bps=p // ATT_BQ),
        grid=(batch, tpb),
        in_specs=[q_spec, k_spec, v_spec, tab_spec],
        out_specs=[o_spec, o_spec],
        out_shape=[out_shape, out_shape],
        compiler_params=pltpu.CompilerParams(
            dimension_semantics=("arbitrary", "arbitrary"), vmem_limit_bytes=VMEM_LIMIT),
        name=f"dilated_attn_g{g}",
    )(qkv5, qkv5, qkv5, tab)


def _dilated_attention(qkv_groups, batch, seq):
    slopes = 2.0 ** (-ALIBI_MAX_EXP * jnp.arange(1, N_ATT_HEADS + 1, dtype=F32) / N_ATT_HEADS)
    slopes = slopes.reshape(N_GROUPS, ATT_HEADS_PER_GROUP)
    outs = []
    for g, (window, dil) in enumerate(ATT_GROUPS):
        assert window // (2 * dil) == ATT_HALF
        outs.append(_attn_group(qkv_groups[g], _attn_bias_tables(slopes[g], dil), g, dil, batch, seq))
    return outs


def _split3(x):
    hi = x.astype(BF16)
    r1 = x - hi.astype(F32)
    mid = r1.astype(BF16)
    lo = (r1 - mid.astype(F32)).astype(BF16)
    return hi, mid, lo


def _hgrn_kernel(hq_ref, hf_ref, hb_ref, hi_ref, hg_ref, lbp_ref, ng_ref, out_ref, o_scr, *, seq):
    c = HGRN_CHUNK
    n_chunks = seq // c
    row = lax.broadcasted_iota(jnp.int32, (c, c), 0)
    col = lax.broadcasted_iota(jnp.int32, (c, c), 1)
    tri = (row >= col).astype(BF16)
    tri3 = jnp.concatenate([tri, tri, tri], axis=1)
    causal = (row >= col, row <= col)
    qscale = HGRN_KEY ** -0.5

    def lower_bound(d):
        a = lbp_ref[d]
        mx = jnp.max(a, axis=0, keepdims=True)
        e = jnp.exp(a - mx)
        return e[0:1] / jnp.sum(e, axis=0, keepdims=True)

    lbs = (lower_bound(0), lower_bound(1))

    def stage_elementwise(d, t):
        r = t["rows"]
        xq = hq_ref[r, :]
        lb = lbs[d]
        f = lb + (1.0 - lb) * jax.nn.sigmoid((hf_ref, hb_ref)[d][r, :])
        t.update(q=xq * jax.nn.sigmoid(xq) * qscale, kk=1.0 - f, lf=jnp.log2(f), v=hi_ref[r, :])

    def stage_prefix(d, t):
        hi, mid, lo = _split3(t["lf"])
        t["pre"] = _dot(tri3, jnp.concatenate([hi, mid, lo], axis=0))

    def stage_decay(d, t):
        pre = t["pre"]
        tot = pre[c - 1:c]
        if d == 0:
            cum, cl = pre, tot
        else:
            cum = tot - pre + t["lf"]
            cl = cum[0:1]
        ec, ecl = jnp.exp2(cum), jnp.exp2(cl)
        kinv = t["kk"] / ec
        t.update(qd=(t["q"] * ec).astype(BF16), ki=kinv.astype(BF16),
                 kd=(kinv * ecl).astype(BF16), dec=ecl)

    def stage_local_dots(d, t):
        t["a"] = _dot_nt(t["qd"], t["ki"])
        t["ut"] = _dot_tn(t["v"], t["kd"])

    def stage_state(d, t, st):
        t["entry"] = st.astype(BF16)
        return st * t["dec"] + t["ut"]

    def stage_output(d, t):
        r, g = t["rows"], t["chunk"]
        o = _dot(jnp.where(causal[d], t["a"], 0.0).astype(BF16), t["v"]) + _dot_nt(t["qd"], t["entry"])
        fwd_first = g < n_chunks - 1 - g
        if (d == 0) == fwd_first:
            o_scr[r, :] = o
        else:
            o = o_scr[r, :] + o
            o = o * lax.rsqrt(jnp.mean(o * o, axis=-1, keepdims=True) + RMS_EPS) * ng_ref[...]
            xg = hg_ref[r, :]
            out_ref[r, :] = (o * (xg * jax.nn.sigmoid(xg))).astype(BF16)

    stages = (stage_elementwise, stage_prefix, stage_decay, stage_local_dots, None, stage_output)

    order = (list(range(n_chunks)), list(reversed(range(n_chunks))))
    tasks = [[{"chunk": g, "rows": slice(g * c, (g + 1) * c)} for g in order[d]] for d in (0, 1)]
    st = [jnp.zeros((HGRN_VAL, HGRN_KEY), F32), jnp.zeros((HGRN_VAL, HGRN_KEY), F32)]
    for slot in range(n_chunks + len(stages) - 1):
        for k, stage in enumerate(stages):
            i = slot - k
            if 0 <= i < n_chunks:
                for d in (0, 1):
                    if stage is None:
                        st[d] = stage_state(d, tasks[d][i], st[d])
                    else:
                        stage(d, tasks[d][i])
                        if stage is stage_output:
                            tasks[d][i].clear()


def _hgrn(proj_a, proj_d, hgrn_lb, norm_g, batch, seq):
    a4 = proj_a.reshape(proj_a.shape[0], batch, seq, PROJ_TN)
    d4 = proj_d.reshape(proj_d.shape[0], batch, seq, PROJ_TN)
    nh = HGRN_HEADS
    hpt = PROJ_TN // LANES
    tpf = D_MODEL // PROJ_TN

    def col(field, tile0=0):
        return pl.BlockSpec((None, None, seq, LANES),
                            lambda b, h: (tile0 + field * tpf + h // hpt, b, 0, h % hpt))

    out = pl.pallas_call(
        functools.partial(_hgrn_kernel, seq=seq),
        grid=(batch, nh),
        in_specs=[
            col(0), col(1), col(2),
            col(0, A_HI_TILE),
            col(3),
            pl.BlockSpec((2, DEPTH + 1, LANES), lambda b, h: (0, 0, h)),
            pl.BlockSpec((1, HGRN_VAL), lambda b, h: (0, 0)),
        ],
        out_specs=pl.BlockSpec((None, seq, LANES), lambda b, h: (b, 0, h)),
        out_shape=jax.ShapeDtypeStruct((batch, seq, D_MODEL), BF16),
        scratch_shapes=[pltpu.VMEM((seq, HGRN_VAL), F32)],
        compiler_params=pltpu.CompilerParams(
            dimension_semantics=("arbitrary", "arbitrary"), vmem_limit_bytes=VMEM_LIMIT),
        name="hgrn2",
    )(d4, d4, d4, a4, d4, hgrn_lb, norm_g)
    return out.reshape(batch * seq, D_MODEL)


def _merge_kernel(x_ref, lg_ref, lb_ref, o0_ref, l0_ref, o1_ref, l1_ref, o2_ref, l2_ref, rec_ref,
                  wg_ref, wa_ref, wr_ref, wo_ref, g1_ref, b1_ref, out_ref):
    tm = MERGE_TM

    def rows_of(ref, r):
        return jnp.concatenate([ref[c, r, :] for c in range(ATT_WIDTH // LANES)], axis=1)

    def attention(r):
        l0, l1, l2 = rows_of(l0_ref, r), rows_of(l1_ref, r), rows_of(l2_ref, r)
        mx = jnp.maximum(jnp.maximum(l0, l1), l2)
        e0, e1, e2 = jnp.exp(l0 - mx), jnp.exp(l1 - mx), jnp.exp(l2 - mx)
        return (e0 * rows_of(o0_ref, r) + e1 * rows_of(o1_ref, r) + e2 * rows_of(o2_ref, r)) / (e0 + e1 + e2)

    rs = tm // MERGE_SUB
    subs = [slice(s * rs, (s + 1) * rs) for s in range(MERGE_SUB)]
    ur = [_dot(rec_ref[r, :], wr_ref[...]) for r in subs]
    h = [_layer_norm(x_ref[r, :], lg_ref[...], lb_ref[...]) for r in subs]
    gates = [_dot(hs.astype(BF16), wg_ref[...]) for hs in h]
    ua = [_dot(attention(r).astype(BF16), wa_ref[...]) for r in subs]
    merged = [jax.nn.sigmoid(g[:, :D_MODEL]) * a + jax.nn.sigmoid(g[:, D_MODEL:]) * u
              for g, a, u in zip(gates, ua, ur)]
    z = [_dot(m.astype(BF16), wo_ref[...]) for m in merged]
    for r, hs, zs in zip(subs, h, z):
        out_ref[r, :] = _layer_norm(DEEPNORM_ALPHA * hs + zs, g1_ref[...], b1_ref[...])


def _const_spec(shape):
    return pl.BlockSpec(shape, lambda i: (0,) * len(shape), pipeline_mode=pl.Buffered(1))


def _merge(x2, lg, lb, att_groups, rec, wg, wa, wr, wo, g1, b1, batch, seq):
    t = x2.shape[0]
    tm = MERGE_TM
    att_spec = pl.BlockSpec((ATT_WIDTH // LANES, tm, LANES), lambda i: (0, i, 0))
    att_args = [a for pair in att_groups for a in pair]
    att_specs = [att_spec] * len(att_args)
    return pl.pallas_call(
        _merge_kernel,
        grid=(t // tm,),
        in_specs=[
            pl.BlockSpec((tm, D_MODEL), lambda i: (i, 0)),
            _const_spec((1, D_MODEL)), _const_spec((1, D_MODEL)),
            *att_specs,
            pl.BlockSpec((tm, D_MODEL), lambda i: (i, 0)),
            _const_spec(wg.shape), _const_spec(wa.shape), _const_spec(wr.shape), _const_spec(wo.shape),
            _const_spec((1, D_MODEL)), _const_spec((1, D_MODEL)),
        ],
        out_specs=pl.BlockSpec((tm, D_MODEL), lambda i: (i, 0)),
        out_shape=jax.ShapeDtypeStruct((t, D_MODEL), F32),
        compiler_params=pltpu.CompilerParams(
            dimension_semantics=("arbitrary",), vmem_limit_bytes=VMEM_LIMIT),
        name="merge_ln1",
    )(x2, lg, lb, *att_args, rec, wg, wa, wr, wo, g1, b1)


def _ffn_kernel(h_ref, wi_ref, wo_ref, g2_ref, b2_ref, out_ref, a_scr):
    for s in range(FFN_TM // FFN_MM):
        r = slice(s * FFN_MM, (s + 1) * FFN_MM)
        h = h_ref[r, :]
        hb = h.astype(BF16)
        for c in range(D_FF // FFN_TC):
            gate = _dot(hb, wi_ref[:, c * FFN_TC:(c + 1) * FFN_TC])
            up = _dot(hb, wi_ref[:, D_FF + c * FFN_TC:D_FF + (c + 1) * FFN_TC])
            a_scr[r, c * FFN_TC:(c + 1) * FFN_TC] = (gate * jax.nn.sigmoid(gate) * up).astype(BF16)
        y = DEEPNORM_ALPHA * h + _dot(a_scr[r, :], wo_ref[...])
        out_ref[r, :] = _layer_norm(y, g2_ref[...], b2_ref[...])


def _ffn(h1, wi, wo, g2, b2):
    t = h1.shape[0]
    tm = FFN_TM
    return pl.pallas_call(
        _ffn_kernel,
        grid=(t // tm,),
        in_specs=[
            pl.BlockSpec((tm, D_MODEL), lambda i: (i, 0)),
            _const_spec(wi.shape), _const_spec(wo.shape),
            _const_spec((1, D_MODEL)), _const_spec((1, D_MODEL)),
        ],
        out_specs=pl.BlockSpec((tm, D_MODEL), lambda i: (i, 0)),
        out_shape=jax.ShapeDtypeStruct((t, D_MODEL), F32),
        scratch_shapes=[pltpu.VMEM((tm, D_FF), BF16)],
        compiler_params=pltpu.CompilerParams(
            dimension_semantics=("arbitrary",), vmem_limit_bytes=VMEM_LIMIT),
        name="ffn_ln2",
    )(h1, wi, wo, g2, b2)


def kernel(x, ln_in_g, ln_in_b, w_in, hgrn_lb, hgrn_norm_g, w_att_up, w_hgrn_up, w_o,
           ln1_g, ln1_b, w_ffn_in, w_ffn_out, ln2_g, ln2_b):
    batch, seq, d = x.shape
    assert d == D_MODEL and w_in.shape == (DEPTH, D_MODEL, IN_COLS)
    assert seq % PROJ_TM == 0 and seq % (16 * ATT_BK) == 0
    x2 = x.reshape(batch * seq, d)
    row = lambda v: v.reshape(1, -1).astype(F32)
    lg, lb = row(ln_in_g), row(ln_in_b)
    w_in_b = w_in[0].astype(BF16)

    proj_a, proj_b, proj_c, proj_d = _ln_proj(x2, lg, lb, w_in_b, batch, seq)
    att_groups = _dilated_attention(
        (proj_a.reshape(proj_a.shape[0], batch, 1, seq, PROJ_TN), proj_b, proj_c), batch, seq)
    rec = _hgrn(proj_a, proj_d, hgrn_lb.astype(F32), row(hgrn_norm_g[0]), batch, seq)
    h1 = _merge(x2, lg, lb, att_groups, rec, w_in_b[:, GATE_COL0:],
                w_att_up[0].astype(BF16), w_hgrn_up[0].astype(BF16), w_o[0].astype(BF16),
                row(ln1_g[0]), row(ln1_b[0]), batch, seq)
    out = _ffn(h1, w_ffn_in[0].astype(BF16), w_ffn_out[0].astype(BF16), row(ln2_g[0]), row(ln2_b[0]))
    return out.reshape(batch, seq, d)
```
